```python
import jax, jax.numpy as jnp
from jax import lax
import numpy as np

D_MODEL = 1024
BATCH = 2
SEQ = 8192
DEPTH = 1

CHUNK = 64
D_MIX = D_MODEL
D_CONV = D_MIX // 2
CONV_WIDTH = 3
N_HEADS = 8
HEAD_DIM = (D_MIX - D_CONV) // N_HEADS
D_ATTN = N_HEADS * HEAD_DIM
ROT_DIM = HEAD_DIM // 4
ROPE_THETA = 500000.0
IDX_HEADS = 4
IDX_DIM = 64
IDX_ROT_DIM = IDX_DIM // 4
INDEX_TOPK = 256
Q_BLOCK = 128
EPS = 1e-6

SPLIT_SIZES = (
    D_CONV, D_CONV, D_CONV, D_CONV,
    D_ATTN, D_ATTN, D_ATTN, D_ATTN,
    IDX_HEADS * IDX_DIM, IDX_DIM, IDX_HEADS,
)
PROJ_DIM = sum(SPLIT_SIZES)

kernel_name = "hybrid_shortconv_dsa_sandwich"


def rms_norm(x, g):
    xf = x.astype(jnp.float32)
    xf = xf * lax.rsqrt(jnp.mean(xf * xf, axis=-1, keepdims=True) + EPS)
    return (xf * g.astype(jnp.float32)).astype(x.dtype)


def rope_partial(x, pos, rot_dim):
    half = rot_dim // 2
    inv_freq = jnp.power(jnp.float32(ROPE_THETA), -jnp.arange(half, dtype=jnp.float32) / half)
    ang = pos.astype(jnp.float32)[:, None] * inv_freq[None, :]
    cos = jnp.cos(ang)[None, :, None, :]
    sin = jnp.sin(ang)[None, :, None, :]
    xr = x[..., :rot_dim].astype(jnp.float32)
    x1, x2 = xr[..., :half], xr[..., half:]
    rot = jnp.concatenate([x1 * cos - x2 * sin, x2 * cos + x1 * sin], axis=-1).astype(x.dtype)
    return jnp.concatenate([rot, x[..., rot_dim:]], axis=-1)


def short_conv_mixer(h, b_gate, c_gate, conv_w):
    u = c_gate * h
    rhs = conv_w.reshape(CONV_WIDTH, 1, D_CONV).astype(u.dtype)
    y = lax.conv_general_dilated(
        u, rhs, window_strides=(1,), padding=[(CONV_WIDTH - 1, 0)],
        dimension_numbers=("NWC", "WIO", "NWC"), feature_group_count=D_CONV)
    return b_gate * y


def sparse_indexed_attention(q, k, v, q_idx, k_idx, w_idx, pos, top_k):
    B, S, H, Dh = q.shape
    nblk = S // Q_BLOCK

    def to_blocks(a):
        return jnp.moveaxis(a.reshape((B, nblk, Q_BLOCK) + a.shape[2:]), 1, 0)

    def block_fn(args):
        qb, qib, wb, posb = args
        limit = (posb // CHUNK + 1) * CHUNK
        admissible = pos[None, :] < limit[:, None]
        logits = jnp.einsum("bqhd,bsd->bqhs", qib, k_idx,
                            preferred_element_type=jnp.float32) * (IDX_DIM ** -0.5)
        score = jnp.einsum("bqh,bqhs->bqs", wb.astype(jnp.float32) * (IDX_HEADS ** -0.5),
                           jax.nn.relu(logits))
        score = jnp.where(admissible[None], score, -jnp.inf)
        _, sel = lax.top_k(score, top_k)
        valid = sel < limit[None, :, None]
        kg = jax.vmap(lambda kk, ii: kk[ii])(k, sel)
        vg = jax.vmap(lambda vv, ii: vv[ii])(v, sel)
        att = jnp.einsum("bqhd,bqkhd->bqhk", qb, kg,
                         preferred_element_type=jnp.float32) * (HEAD_DIM ** -0.5)
        att = jnp.where(valid[:, :, None, :], att, -jnp.inf)
        p = jax.nn.softmax(att, axis=-1)
        return jnp.einsum("bqhk,bqkhd->bqhd", p.astype(vg.dtype), vg)

    out = lax.map(block_fn, (to_blocks(q), to_blocks(q_idx), to_blocks(w_idx),
                             pos.reshape(nblk, Q_BLOCK)))
    return jnp.moveaxis(out, 0, 1).reshape(B, S, H, Dh)


def setup_inputs(seed: int = 0) -> dict:
    key = jax.random.key(seed)
    ks = jax.random.split(key, 7)
    x = jax.random.normal(ks[0], (BATCH, SEQ, D_MODEL), jnp.float32)
    norm_pre_g = 1.0 + 0.05 * jax.random.normal(ks[1], (DEPTH, D_MODEL), jnp.float32)
    w_in = jax.random.normal(ks[2], (DEPTH, D_MODEL, PROJ_DIM), jnp.float32) * (D_MODEL ** -0.5)
    conv_w = jax.random.normal(ks[3], (DEPTH, CONV_WIDTH, D_CONV), jnp.float32) * (CONV_WIDTH ** -0.5)
    w_out = jax.random.normal(ks[4], (DEPTH, D_MIX, D_MODEL), jnp.float32) * (D_MIX ** -0.5)
    norm_post_g = 1.0 + 0.05 * jax.random.normal(ks[5], (DEPTH, D_MODEL), jnp.float32)
    return {"x": x, "norm_pre_g": norm_pre_g, "w_in": w_in, "conv_w": conv_w,
            "w_out": w_out, "norm_post_g": norm_post_g}


def reference(x, norm_pre_g, w_in, conv_w, w_out, norm_post_g):
    B, S, _ = x.shape
    top_k = min(INDEX_TOPK, S // 4)
    pos = jnp.arange(S, dtype=jnp.int32)
    split_idx = [int(i) for i in np.cumsum(SPLIT_SIZES)[:-1]]
    for l in range(DEPTH):
        xn = rms_norm(x, norm_pre_g[l])
        proj = jnp.einsum("bsd,dp->bsp", xn, w_in[l])
        (h, b_gate, c_gate, z_conv, q, k, v, z_attn,
         q_idx, k_idx, w_idx) = jnp.split(proj, split_idx, axis=-1)

        y_conv = short_conv_mixer(h, b_gate, c_gate, conv_w[l]) * jax.nn.silu(z_conv)

        q = rope_partial(q.reshape(B, S, N_HEADS, HEAD_DIM), pos, ROT_DIM)
        k = rope_partial(k.reshape(B, S, N_HEADS, HEAD_DIM), pos, ROT_DIM)
        v = v.reshape(B, S, N_HEADS, HEAD_DIM)
        q_idx = rope_partial(q_idx.reshape(B, S, IDX_HEADS, IDX_DIM), pos, IDX_ROT_DIM)
        k_idx = rope_partial(k_idx[:, :, None, :], pos, IDX_ROT_DIM)[:, :, 0, :]
        o = sparse_indexed_attention(q, k, v, q_idx, k_idx, w_idx, pos, top_k)
        y_attn = o.reshape(B, S, D_ATTN) * jax.nn.silu(z_attn)

        y = jnp.einsum("bsm,md->bsd", jnp.concatenate([y_conv, y_attn], axis=-1), w_out[l])
        x = x + rms_norm(y, norm_post_g[l])
    return x
```

```python
import functools

import jax
import jax.numpy as jnp
import numpy as np
from jax import lax
from jax.experimental import pallas as pl
from jax.experimental.pallas import tpu as pltpu

D_MODEL = 1024
CHUNK = 64
D_CONV = 512
CONV_WIDTH = 3
N_HEADS = 8
HEAD_DIM = 64
D_ATTN = N_HEADS * HEAD_DIM
ROT_DIM = HEAD_DIM // 4
ROPE_THETA = 500000.0
IDX_HEADS = 4
IDX_DIM = 64
INDEX_TOPK = 256
EPS = 1e-6

LANES = 128
T_PROJ = 256
T_Q = 256
S_C = 256
T_OUT = 512
IDX_W = 512
N_BISECT = 24
NEG = -1e30
VMEM_LIMIT = 56 * 1024 * 1024

f32 = jnp.float32
bf16 = jnp.bfloat16


def _rope128(xb, c, sa, sb):
    return xb * c + pltpu.roll(xb, LANES - ROT_DIM // 2, 1) * sa + pltpu.roll(xb, ROT_DIM // 2, 1) * sb


def _proj_kernel(x_ref, g_ref, wc_ref, wa_ref, wi_ref, cw_ref, c_ref, sa_ref, sb_ref,
                 yc_ref, q_ref, kt_ref, v_ref, gz_ref, qi_ref, kit_ref, sg_ref, ubuf):
    j = pl.program_id(1)
    T = T_PROJ

    @pl.when(j == 0)
    def _():
        ubuf[0:8, :] = jnp.zeros((8, D_CONV), f32)

    x = x_ref[0]
    xn = x * lax.rsqrt(jnp.mean(x * x, axis=-1, keepdims=True) + EPS) * g_ref[...]
    xn = xn.astype(bf16)

    def proj(w_ref, k):
        return jnp.dot(xn, w_ref[:, k * 512:(k + 1) * 512], preferred_element_type=f32)

    u = proj(wc_ref, 2) * proj(wc_ref, 0)
    ubuf[8:8 + T, :] = u
    conv = (ubuf[6:6 + T, :] * cw_ref[0:1, :] + ubuf[7:7 + T, :] * cw_ref[1:2, :] + u * cw_ref[2:3, :])
    ubuf[0:8, :] = ubuf[T:T + 8, :]
    zc = proj(wc_ref, 3)
    yc_ref[0] = (proj(wc_ref, 1) * conv * (zc * jax.nn.sigmoid(zc))).astype(bf16)

    c = c_ref[...]
    sa = sa_ref[...]
    sb = sb_ref[...]

    qf = proj(wa_ref, 0)
    kf = proj(wa_ref, 1)
    for i in range(D_ATTN // LANES):
        sl = slice(i * LANES, (i + 1) * LANES)
        q_ref[0, :, sl] = (_rope128(qf[:, sl], c, sa, sb) * (HEAD_DIM ** -0.5)).astype(bf16)
        kt_ref[0, 0, sl, :] = _rope128(kf[:, sl], c, sa, sb).T.astype(bf16)
    v_ref[0] = proj(wa_ref, 2).astype(bf16)
    za = proj(wa_ref, 3)
    gz_ref[0] = za * jax.nn.sigmoid(za)

    pi = jnp.dot(xn, wi_ref[...], preferred_element_type=f32)
    wv = pi[:, 3 * LANES:4 * LANES]
    sg_ref[0] = jnp.where(wv > 0, 1.0, jnp.where(wv < 0, -1.0, 0.0))
    wabs = jnp.abs(wv) * ((IDX_DIM ** -0.5) * (IDX_HEADS ** -0.5))
    lane = lax.broadcasted_iota(jnp.int32, (T, LANES), 1)
    for i in range(2):
        sl = slice(i * LANES, (i + 1) * LANES)
        scale = jnp.where(lane < IDX_DIM, wabs[:, 2 * i:2 * i + 1], wabs[:, 2 * i + 1:2 * i + 2])
        qi_ref[0, :, sl] = (_rope128(pi[:, sl], c, sa, sb) * scale).astype(bf16)
    kit = _rope128(pi[:, 2 * LANES:3 * LANES], c, sa, sb).T
    kit_ref[0, 0] = kit[0:IDX_DIM, :].astype(bf16)


def _attn_kernel(q_ref, qi_ref, sg_ref, gz_ref, kt_ref, v_ref, kit_ref, o_ref, sc_ref, *, top_k):
    i = pl.program_id(1)
    TQ, SC = T_Q, S_C
    nfull = i

    row = lax.broadcasted_iota(jnp.int32, (TQ, 1), 0)
    lim_local = ((row // CHUNK) + 1) * CHUNK
    limit = lim_local + i * TQ
    keff = jnp.minimum(limit, top_k).astype(f32)
    col = lax.broadcasted_iota(jnp.int32, (TQ, SC), 1)
    adm_diag = col < lim_local

    sg = sg_ref[0]
    sgb = [jnp.broadcast_to(sg[:, h:h + 1], (TQ, SC)) for h in range(IDX_HEADS)]
    qi = [qi_ref[0, :, h * IDX_DIM:(h + 1) * IDX_DIM] for h in range(IDX_HEADS)]

    def score_chunk(j):
        kc = kit_ref[0, j]
        s = None
        for h in range(IDX_HEADS):
            l = jnp.dot(qi[h], kc, preferred_element_type=f32)
            t = jnp.maximum(l, 0.0) * sgb[h]
            s = t if s is None else s + t
        return s

    def fold_min(a):
        return jnp.minimum(a[:, 0:LANES], a[:, LANES:2 * LANES])

    def fold_max(a):
        return jnp.maximum(a[:, 0:LANES], a[:, LANES:2 * LANES])

    def p1_body(j, carry):
        mn, mx = carry
        s = score_chunk(j)
        sc_ref[j] = s
        return jnp.minimum(mn, fold_min(s)), jnp.maximum(mx, fold_max(s))

    mn0 = jnp.full((TQ, LANES), jnp.inf, f32)
    mx0 = jnp.full((TQ, LANES), -jnp.inf, f32)
    mn, mx = lax.fori_loop(0, nfull, p1_body, (mn0, mx0))
    s = score_chunk(i)
    sc_ref[i] = jnp.where(adm_diag, s, -jnp.inf)
    mn = jnp.minimum(mn, fold_min(jnp.where(adm_diag, s, jnp.inf)))
    mx = jnp.maximum(mx, fold_max(jnp.where(adm_diag, s, -jnp.inf)))
    lo = jnp.min(mn, axis=1, keepdims=True)
    hi = jnp.max(mx, axis=1, keepdims=True)

    def count_ge(mid):
        def body(j, cnt):
            m = jnp.where(sc_ref[j] >= mid, 1.0, 0.0)
            return cnt + (m[:, 0:LANES] + m[:, LANES:2 * LANES])
        cnt = lax.fori_loop(0, i + 1, body, jnp.zeros((TQ, LANES), f32))
        return jnp.sum(cnt, axis=1, keepdims=True)

    def bisect(state):
        lo, hi, clo = state
        mid = 0.5 * lo + 0.5 * hi
        c = count_ge(mid)
        ge = c >= keff
        return jnp.where(ge, mid, lo), jnp.where(ge, hi, mid), jnp.where(ge, c, clo)

    def snap(lo, hi):
        def body(j, carry):
            a, b = carry
            s = sc_ref[j]
            a = jnp.minimum(a, fold_min(jnp.where(s >= lo, s, jnp.inf)))
            b = jnp.maximum(b, fold_max(jnp.where(s <= hi, s, -jnp.inf)))
            return a, b
        a, b = lax.fori_loop(0, i + 1, body, (mn0, mx0))
        return jnp.min(a, axis=1, keepdims=True), jnp.max(b, axis=1, keepdims=True)

    clo = limit.astype(f32)
    lo, hi, clo = lax.fori_loop(0, N_BISECT, lambda _, st: bisect(st), (lo, hi, clo))
    vlo, vhi = snap(lo, hi)

    def unconverged(lo, hi, clo, vlo, vhi):
        bad = jnp.logical_and(clo != keff, vlo != vhi)
        return jnp.max(jnp.where(bad, 1.0, 0.0)) > 0.0

    def refine(state):
        lo, hi, clo, _, _ = state
        lo, hi, clo = bisect((lo, hi, clo))
        vlo, vhi = snap(lo, hi)
        return lo, hi, clo, vlo, vhi

    lo, hi, clo, vlo, vhi = lax.while_loop(lambda st: unconverged(*st), refine, (lo, hi, clo, vlo, vhi))
    tau = vlo
    excess = clo - keff

    ra = lax.broadcasted_iota(jnp.int32, (SC, SC), 0)
    rb = lax.broadcasted_iota(jnp.int32, (SC, SC), 1)
    after = jnp.where(ra > rb, 1.0, 0.0).astype(bf16)

    def p3_body(jj, carry):
        j = i - jj
        s = sc_ref[j]
        eq = s == tau
        eqf = jnp.where(eq, 1.0, 0.0)
        later = jnp.dot(eqf.astype(bf16), after, preferred_element_type=f32) + carry
        sel = jnp.logical_or(s > tau, jnp.logical_and(eq, later >= excess))
        sc_ref[j] = jnp.where(sel, 0.0, NEG)
        return carry + jnp.sum(eqf, axis=1, keepdims=True)

    lax.fori_loop(0, i + 1, p3_body, jnp.zeros((TQ, 1), f32))

    lane = lax.broadcasted_iota(jnp.int32, (TQ, LANES), 1)
    for hp in range(N_HEADS // 2):
        outs = []
        for h in (2 * hp, 2 * hp + 1):
            qh = q_ref[0, :, h * HEAD_DIM:(h + 1) * HEAD_DIM]

            def p4_body(j, carry, h=h, hp=hp, qh=qh):
                m, l, acc = carry
                kc = kt_ref[0, j, h * HEAD_DIM:(h + 1) * HEAD_DIM, :]
                a = jnp.dot(qh, kc, preferred_element_type=f32) + sc_ref[j]
                m_new = jnp.maximum(m, jnp.max(a, axis=1, keepdims=True))
                alpha = jnp.exp(m - m_new)
                p = jnp.exp(a - m_new)
                l = alpha * l + jnp.sum(p, axis=1, keepdims=True)
                vc = v_ref[0, pl.ds(pl.multiple_of(j * SC, SC), SC), hp * LANES:(hp + 1) * LANES]
                acc = alpha * acc + jnp.dot(p.astype(bf16), vc, preferred_element_type=f32)
                return m_new, l, acc

            m0 = jnp.full((TQ, 1), NEG, f32)
            l0 = jnp.zeros((TQ, 1), f32)
            a0 = jnp.zeros((TQ, LANES), f32)
            _, l, acc = lax.fori_loop(0, i + 1, p4_body, (m0, l0, a0))
            outs.append(acc / l)
        o_pair = jnp.where(lane < HEAD_DIM, outs[0], outs[1])
        sl = slice(hp * LANES, (hp + 1) * LANES)
        o_ref[0, :, sl] = (o_pair * gz_ref[0, :, sl]).astype(bf16)


def _out_kernel(x_ref, yc_ref, ya_ref, woc_ref, woa_ref, g_ref, o_ref):
    y = (jnp.dot(yc_ref[0], woc_ref[...], preferred_element_type=f32)
         + jnp.dot(ya_ref[0], woa_ref[...], preferred_element_type=f32))
    yn = y * lax.rsqrt(jnp.mean(y * y, axis=-1, keepdims=True) + EPS) * g_ref[...]
    o_ref[0] = x_ref[0] + yn


def _rope_tables(S):
    half = ROT_DIM // 2
    inv_freq = jnp.power(jnp.float32(ROPE_THETA), -jnp.arange(half, dtype=f32) / half)
    ang = jnp.arange(S, dtype=jnp.int32).astype(f32)[:, None] * inv_freq[None, :]
    cos, sin = jnp.cos(ang), jnp.sin(ang)
    ones = jnp.ones((S, HEAD_DIM - ROT_DIM), f32)
    zeros = jnp.zeros((S, HEAD_DIM - ROT_DIM), f32)
    zh = jnp.zeros((S, half), f32)
    c = jnp.concatenate([cos, cos, ones], axis=1)
    sa = jnp.concatenate([-sin, zh, zeros], axis=1)
    sb = jnp.concatenate([zh, sin, zeros], axis=1)
    rep = LANES // HEAD_DIM
    return jnp.tile(c, (1, rep)), jnp.tile(sa, (1, rep)), jnp.tile(sb, (1, rep))


def _const_spec(shape):
    nd = len(shape)
    return pl.BlockSpec(shape, lambda b, j: (0,) * nd, pipeline_mode=pl.Buffered(1))


def _layer(x, g_pre, w_in, conv_w, w_out, g_post):
    B, S, _ = x.shape
    assert S % T_PROJ == 0 and S % T_Q == 0 and S % T_OUT == 0 and T_Q == S_C == T_PROJ
    top_k = min(INDEX_TOPK, S // 4)
    nc = S // S_C

    wc = w_in[:, 0:4 * D_CONV].astype(bf16)
    wa = w_in[:, 4 * D_CONV:4 * D_CONV + 4 * D_ATTN].astype(bf16)
    o = 4 * D_CONV + 4 * D_ATTN
    nq = IDX_HEADS * IDX_DIM
    wi = jnp.zeros((D_MODEL, IDX_W), f32)
    wi = wi.at[:, 0:nq + IDX_DIM].set(w_in[:, o:o + nq + IDX_DIM])
    wi = wi.at[:, 3 * LANES:3 * LANES + IDX_HEADS].set(w_in[:, o + nq + IDX_DIM:o + nq + IDX_DIM + IDX_HEADS])
    wi = wi.astype(bf16)
    woc = w_out[0:D_CONV].astype(bf16)
    woa = w_out[D_CONV:].astype(bf16)
    c_t, sa_t, sb_t = _rope_tables(S)

    cp = functools.partial(pltpu.CompilerParams, vmem_limit_bytes=VMEM_LIMIT)

    tile = lambda w: pl.BlockSpec((1, T_PROJ, w), lambda b, j: (b, j, 0))
    tab = pl.BlockSpec((T_PROJ, LANES), lambda b, j: (j, 0))
    yc, q, kt, v, gz, qi, kit, sg = pl.pallas_call(
        _proj_kernel,
        grid=(B, S // T_PROJ),
        in_specs=[tile(D_MODEL), _const_spec((1, D_MODEL)), _const_spec((D_MODEL, 4 * D_CONV)),
                  _const_spec((D_MODEL, 4 * D_ATTN)), _const_spec((D_MODEL, IDX_W)),
                  _const_spec((CONV_WIDTH, D_CONV)), tab, tab, tab],
        out_specs=[tile(D_CONV), tile(D_ATTN),
                   pl.BlockSpec((1, 1, D_ATTN, T_PROJ), lambda b, j: (b, j, 0, 0)),
                   tile(D_ATTN), tile(D_ATTN), tile(nq),
                   pl.BlockSpec((1, 1, IDX_DIM, T_PROJ), lambda b, j: (b, j, 0, 0)),
                   tile(LANES)],
        out_shape=[jax.ShapeDtypeStruct((B, S, D_CONV), bf16), jax.ShapeDtypeStruct((B, S, D_ATTN), bf16),
                   jax.ShapeDtypeStruct((B, nc, D_ATTN, S_C), bf16), jax.ShapeDtypeStruct((B, S, D_ATTN), bf16),
                   jax.ShapeDtypeStruct((B, S, D_ATTN), f32), jax.ShapeDtypeStruct((B, S, nq), bf16),
                   jax.ShapeDtypeStruct((B, nc, IDX_DIM, S_C), bf16), jax.ShapeDtypeStruct((B, S, LANES), f32)],
        scratch_shapes=[pltpu.VMEM((T_PROJ + 8, D_CONV), f32)],
        compiler_params=cp(dimension_semantics=("arbitrary", "arbitrary")),
        name="proj",
    )(x, g_pre.reshape(1, D_MODEL), wc, wa, wi, conv_w, c_t, sa_t, sb_t)

    qtile = lambda w: pl.BlockSpec((1, T_Q, w), lambda b, i: (b, i, 0))
    whole = lambda shape: pl.BlockSpec((1,) + shape, lambda b, i: (b,) + (0,) * len(shape),
                                       pipeline_mode=pl.Buffered(1))
    ya = pl.pallas_call(
        functools.partial(_attn_kernel, top_k=top_k),
        grid=(B, S // T_Q),
        in_specs=[qtile(D_ATTN), qtile(nq), qtile(LANES), qtile(D_ATTN),
                  whole((nc, D_ATTN, S_C)), whole((S, D_ATTN)), whole((nc, IDX_DIM, S_C))],
        out_specs=qtile(D_ATTN),
        out_shape=jax.ShapeDtypeStruct((B, S, D_ATTN), bf16),
        scratch_shapes=[pltpu.VMEM((nc, T_Q, S_C), f32)],
        compiler_params=cp(dimension_semantics=("arbitrary", "arbitrary")),
        name="attn",
    )(q, qi, sg, gz, kt, v, kit)

    otile = lambda w: pl.BlockSpec((1, T_OUT, w), lambda b, j: (b, j, 0))
    return pl.pallas_call(
        _out_kernel,
        grid=(B, S // T_OUT),
        in_specs=[otile(D_MODEL), otile(D_CONV), otile(D_ATTN), _const_spec((D_CONV, D_MODEL)),
                  _const_spec((D_ATTN, D_MODEL)), _const_spec((1, D_MODEL))],
        out_specs=otile(D_MODEL),
        out_shape=jax.ShapeDtypeStruct((B, S, D_MODEL), f32),
        compiler_params=cp(dimension_semantics=("arbitrary", "arbitrary")),
        name="outproj",
    )(x, yc, ya, woc, woa, g_post.reshape(1, D_MODEL))


def kernel(x, norm_pre_g, w_in, conv_w, w_out, norm_post_g):
    for l in range(norm_pre_g.shape[0]):
        x = _layer(x, norm_pre_g[l], w_in[l], conv_w[l], w_out[l], norm_post_g[l])
    return x
```

```python
import functools

import jax
import jax.numpy as jnp
from jax import lax
from jax.experimental import pallas as pl
from jax.experimental.pallas import tpu as pltpu

D_MODEL = 1024
CHUNK = 64
D_CONV = 512
CONV_WIDTH = 3
N_HEADS = 8
HEAD_DIM = 64
D_ATTN = N_HEADS * HEAD_DIM
ROT_DIM = HEAD_DIM // 4
ROPE_THETA = 500000.0
IDX_HEADS = 4
IDX_DIM = 64
INDEX_TOPK = 256
EPS = 1e-6

LANES = 128
SUBLANES = 8
T_PROJ = 256
T_Q = 256
S_C = 256
T_OUT = 512
IDX_W = 512
N_BISECT = 24
NEG = -1e30
RIDE = 16
V_ROWS = HEAD_DIM + 16
Q_SCALE = HEAD_DIM ** -0.5 * 1.4426950408889634
VMEM_LIMIT = 56 * 1024 * 1024

f32 = jnp.float32
bf16 = jnp.bfloat16


def _rope128(xb, c, sa, sb):
    return xb * c + pltpu.roll(xb, LANES - ROT_DIM // 2, 1) * sa + pltpu.roll(xb, ROT_DIM // 2, 1) * sb


def _proj_kernel(x_ref, g_ref, wc_ref, wa_ref, wi_ref, cw_ref, c_ref, sa_ref, sb_ref,
                 yc_ref, qtz_ref, k_ref, vta_ref, gz_ref, qit_ref, ki_ref, sgt_ref, ubuf):
    j = pl.program_id(1)
    T = T_PROJ

    @pl.when(j == 0)
    def _():
        ubuf[0:8, :] = jnp.zeros((8, D_CONV), f32)

    x = x_ref[0]
    xn = x * lax.rsqrt(jnp.mean(x * x, axis=-1, keepdims=True) + EPS) * g_ref[...]
    xn = xn.astype(bf16)

    def proj(w_ref, k):
        return jnp.dot(xn, w_ref[:, k * 512:(k + 1) * 512], preferred_element_type=f32)

    u = proj(wc_ref, 2) * proj(wc_ref, 0)
    ubuf[8:8 + T, :] = u
    conv = (ubuf[6:6 + T, :] * cw_ref[0:1, :] + ubuf[7:7 + T, :] * cw_ref[1:2, :] + u * cw_ref[2:3, :])
    ubuf[0:8, :] = ubuf[T:T + 8, :]
    zc = proj(wc_ref, 3)
    yc_ref[0] = (proj(wc_ref, 1) * conv * (zc * jax.nn.sigmoid(zc))).astype(bf16)

    c = c_ref[...]
    sa = sa_ref[...]
    sb = sb_ref[...]

    qf = proj(wa_ref, 0)
    kf = proj(wa_ref, 1)
    vf = proj(wa_ref, 2)
    sub = lax.broadcasted_iota(jnp.int32, (LANES, T), 0)
    for i in range(D_ATTN // LANES):
        sl = slice(i * LANES, (i + 1) * LANES)
        qt = (_rope128(qf[:, sl], c, sa, sb) * Q_SCALE).T
        qtz_ref[0, 0, 2 * i] = jnp.where(sub < HEAD_DIM, qt, 0.0).astype(bf16)
        qtz_ref[0, 0, 2 * i + 1] = jnp.where(sub >= HEAD_DIM, qt, 0.0).astype(bf16)
        k_ref[0, i] = _rope128(kf[:, sl], c, sa, sb).astype(bf16)
        vt = vf[:, sl].T.astype(bf16)
        for h in (2 * i, 2 * i + 1):
            vta_ref[0, 0, h, 0:HEAD_DIM, :] = vt[(h % 2) * HEAD_DIM:(h % 2 + 1) * HEAD_DIM, :]
            vta_ref[0, 0, h, HEAD_DIM:V_ROWS, :] = jnp.ones((V_ROWS - HEAD_DIM, T), bf16)
    za = proj(wa_ref, 3)
    gz_ref[0] = za * jax.nn.sigmoid(za)

    pi = jnp.dot(xn, wi_ref[...], preferred_element_type=f32)
    wv = pi[:, 3 * LANES:4 * LANES]
    sgn = jnp.where(wv > 0, 1.0, jnp.where(wv < 0, -1.0, 0.0))
    sgt_ref[0, 0] = sgn.T[0:SUBLANES, :]
    wabs = jnp.abs(wv) * ((IDX_DIM ** -0.5) * (IDX_HEADS ** -0.5))
    lane = lax.broadcasted_iota(jnp.int32, (T, LANES), 1)
    for i in range(2):
        sl = slice(i * LANES, (i + 1) * LANES)
        scale = jnp.where(lane < IDX_DIM, wabs[:, 2 * i:2 * i + 1], wabs[:, 2 * i + 1:2 * i + 2])
        qit_ref[0, 0, sl, :] = (_rope128(pi[:, sl], c, sa, sb) * scale).T.astype(bf16)
    ki_ref[0] = _rope128(pi[:, 2 * LANES:3 * LANES], c, sa, sb)[:, 0:IDX_DIM].astype(bf16)


def _attn_kernel(qtz_ref, qit_ref, sgt_ref, gz_ref, k_ref, vta_ref, ki_ref, o_ref,
                 sc_ref, acc_ref, m_ref, l_ref, a0_ref, a1_ref, *, top_k, n_tiles):
    g = pl.program_id(1)
    TQ, SC = T_Q, S_C
    G = SC // SUBLANES
    nslot = sc_ref.shape[0] // 2
    nc = nslot - 1
    i = jnp.minimum(g, n_tiles - 1)
    ip = g - 1
    live = g < n_tiles
    cur = (g % 2) * nslot
    prev = nslot - cur

    def fold(a, op):
        return op(a.reshape(G, SUBLANES, TQ), axis=0)

    def rows(j):
        return pl.ds(pl.multiple_of(j * SC, SC), SC)

    def trips(n):
        return jnp.where(live, n, 0)

    t_loc = lax.broadcasted_iota(jnp.int32, (1, TQ), 1)
    lim_local = ((t_loc // CHUNK) + 1) * CHUNK
    limit = lim_local + i * TQ
    keff = jnp.minimum(limit, top_k).astype(f32)
    adm_diag = lax.broadcasted_iota(jnp.int32, (SC, TQ), 0) < lim_local

    sgt = sgt_ref[0, 0]
    sg = [sgt[h:h + 1, :] for h in range(IDX_HEADS)]

    def score_chunk(j):
        kc = ki_ref[0, rows(j), :]
        s = None
        for h in range(IDX_HEADS):
            l = jnp.dot(kc, qit_ref[0, 0, h * IDX_DIM:(h + 1) * IDX_DIM, :], preferred_element_type=f32)
            t = jnp.maximum(l, 0.0) * sg[h]
            s = t if s is None else s + t
        return s

    minus_inf = jnp.full((SC, TQ), -jnp.inf, f32)
    sc_ref[cur + nc] = minus_inf
    sc_ref[prev + nc] = minus_inf

    def p1_body(jj, carry):
        mn, mx = carry
        for j in (2 * jj, jnp.minimum(2 * jj + 1, i - 1)):
            s = score_chunk(j)
            sc_ref[cur + j] = s
            mn, mx = jnp.minimum(mn, fold(s, jnp.min)), jnp.maximum(mx, fold(s, jnp.max))
        return mn, mx

    mn0 = jnp.full((SUBLANES, TQ), jnp.inf, f32)
    mx0 = jnp.full((SUBLANES, TQ), -jnp.inf, f32)
    mn, mx = lax.fori_loop(0, trips((i + 1) // 2), p1_body, (mn0, mx0))
    s = score_chunk(i)
    sc_ref[cur + i] = jnp.where(adm_diag, s, -jnp.inf)
    mn = jnp.minimum(mn, fold(jnp.where(adm_diag, s, jnp.inf), jnp.min))
    mx = jnp.maximum(mx, fold(jnp.where(adm_diag, s, -jnp.inf), jnp.max))
    lo = jnp.min(mn, axis=0, keepdims=True)
    hi = jnp.max(mx, axis=0, keepdims=True)

    acc_ref[...] = jnp.zeros(acc_ref.shape, f32)
    m_ref[...] = jnp.full(m_ref.shape, NEG, f32)
    l_ref[...] = jnp.zeros(l_ref.shape, f32)
    P = 2 * SUBLANES

    def all_sublanes(x8, op):
        for shift in (4, 2, 1):
            x8 = op(x8, pltpu.roll(x8, shift, 0))
        return x8

    def kv_chunk(c):
        return jnp.clip(c, 0, jnp.maximum(ip, 0))

    def logits_to(buf, c, between=lambda h: None):
        bias = sc_ref[jnp.where(c <= ip, prev + c, prev + nc)]
        for h in range(N_HEADS):
            kc = k_ref[0, h // 2, rows(kv_chunk(c)), :]
            buf[h] = (jnp.dot(kc, qtz_ref[0, 0, h], preferred_element_type=f32) + bias).astype(bf16)
            between(h)

    def softmax_pv(buf, c, between=lambda h: None):
        for h in range(N_HEADS):
            a = buf[h]
            cmax = jnp.max(a.reshape(SC // P, P, TQ), axis=0).astype(f32)
            m_old = m_ref[h]
            m_new = jnp.maximum(m_old, all_sublanes(jnp.maximum(cmax[0:SUBLANES], cmax[SUBLANES:P]), jnp.maximum))
            alpha = jnp.exp2(m_old - m_new)
            m_ref[h] = m_new
            mb = jnp.concatenate([m_new, m_new], axis=0).astype(bf16)
            p = jnp.exp2(a.reshape(SC // P, P, TQ) - mb[None]).reshape(SC, TQ)
            pv = jnp.dot(vta_ref[0, kv_chunk(c), h], p, preferred_element_type=f32)
            l_ref[h] = alpha * l_ref[h] + pv[HEAD_DIM:HEAD_DIM + SUBLANES]
            acc = acc_ref[h].reshape(HEAD_DIM // SUBLANES, SUBLANES, TQ) * alpha[None]
            acc_ref[h] = acc.reshape(HEAD_DIM, TQ) + pv[0:HEAD_DIM]
            between(h)

    def chunk_count(slot, mid8):
        return jnp.sum(jnp.where(sc_ref[slot].reshape(G, SUBLANES, TQ) >= mid8[None], 1.0, 0.0), axis=0)

    def update(state, mid, c, take=True):
        lo, hi, clo = state
        ge = c >= keff
        up = jnp.logical_and(ge, take)
        down = jnp.logical_and(jnp.logical_not(ge), take)
        return jnp.where(up, mid, lo), jnp.where(down, mid, hi), jnp.where(up, c, clo)

    n = i + 1
    n_trips = (g + 1) // 2
    per_step = (n + RIDE - 1) // RIDE
    n_ride = jnp.where(live, jnp.minimum(n_trips // per_step, N_BISECT), 0)
    logits_to(a0_ref, 0)

    def p4_body(jj, carry):
        lo, hi, clo, cnt, cursor, done = carry
        j = 2 * jj
        active = done < n_ride
        mid = 0.5 * lo + 0.5 * hi
        mid8 = jnp.broadcast_to(mid, (SUBLANES, TQ))

        counts = [cnt]

        def ride(stage):
            per_stage = RIDE // 4

            def between(h):
                for t in range(h * per_stage // N_HEADS, (h + 1) * per_stage // N_HEADS):
                    c = cursor + stage * per_stage + t
                    slot = jnp.where(jnp.logical_and(c <= i, active), cur + c, cur + nc)
                    counts[0] = counts[0] + chunk_count(slot, mid8)
            return between

        logits_to(a1_ref, j + 1, ride(0))
        softmax_pv(a0_ref, j, ride(1))
        logits_to(a0_ref, j + 2, ride(2))
        softmax_pv(a1_ref, j + 1, ride(3))
        cnt = counts[0]
        cursor = cursor + RIDE
        full = jnp.logical_and(cursor >= n, active)
        lo, hi, clo = update((lo, hi, clo), mid, jnp.sum(cnt, axis=0, keepdims=True), take=full)
        keep = jnp.where(full, 0.0, 1.0)
        return lo, hi, clo, cnt * keep, jnp.where(full, 0, cursor), done + full.astype(jnp.int32)

    clo = limit.astype(f32)
    lo, hi, clo, _, _, done = lax.fori_loop(
        0, n_trips, p4_body, (lo, hi, clo, jnp.zeros((SUBLANES, TQ), f32), jnp.int32(0), jnp.int32(0)))

    @pl.when(g >= 1)
    def _():
        for hp in range(N_HEADS // 2):
            pair = jnp.concatenate(
                [(acc_ref[h].reshape(HEAD_DIM // SUBLANES, SUBLANES, TQ) / l_ref[h][None]).reshape(HEAD_DIM, TQ)
                 for h in (2 * hp, 2 * hp + 1)], axis=0)
            sl = slice(hp * LANES, (hp + 1) * LANES)
            o_ref[0, :, sl] = (pair.T * gz_ref[0, :, sl]).astype(bf16)

    def count_ge(mid):
        mid8 = jnp.broadcast_to(mid, (SUBLANES, TQ))

        def body(jj, cnt):
            j2 = 2 * jj + 1
            return (cnt + chunk_count(cur + 2 * jj, mid8)
                    + chunk_count(jnp.where(j2 <= i, cur + j2, cur + nc), mid8))
        cnt = lax.fori_loop(0, i // 2 + 1, body, jnp.zeros((SUBLANES, TQ), f32))
        return jnp.sum(cnt, axis=0, keepdims=True)

    def bisect(state):
        mid = 0.5 * state[0] + 0.5 * state[1]
        return update(state, mid, count_ge(mid))

    def snap(lo, hi):
        def body(jj, carry):
            a, b = carry
            for j in (2 * jj, jnp.minimum(2 * jj + 1, i)):
                s = sc_ref[cur + j]
                a = jnp.minimum(a, fold(jnp.where(s >= lo, s, jnp.inf), jnp.min))
                b = jnp.maximum(b, fold(jnp.where(s <= hi, s, -jnp.inf), jnp.max))
            return a, b
        a, b = lax.fori_loop(0, trips(i // 2 + 1), body, (mn0, mx0))
        return jnp.min(a, axis=0, keepdims=True), jnp.max(b, axis=0, keepdims=True)

    lo, hi, clo = lax.fori_loop(0, trips(N_BISECT - done), lambda _, st: bisect(st), (lo, hi, clo))
    vlo, vhi = snap(lo, hi)

    def unconverged(lo, hi, clo, vlo, vhi):
        bad = jnp.logical_and(clo != keff, vlo != vhi)
        return jnp.logical_and(live, jnp.max(jnp.where(bad, 1.0, 0.0)) > 0.0)

    def refine(state):
        lo, hi, clo, _, _ = state
        lo, hi, clo = bisect((lo, hi, clo))
        vlo, vhi = snap(lo, hi)
        return lo, hi, clo, vlo, vhi

    lo, hi, clo, vlo, vhi = lax.while_loop(lambda st: unconverged(*st), refine, (lo, hi, clo, vlo, vhi))
    tau = vlo
    excess = clo - keff

    ka = lax.broadcasted_iota(jnp.int32, (SC, SC), 0)
    kb = lax.broadcasted_iota(jnp.int32, (SC, SC), 1)
    upper = jnp.where(kb > ka, 1.0, 0.0).astype(bf16)

    def p3_body(jj, carry):
        ja = i - 2 * jj
        slots = (cur + ja, jnp.where(ja >= 1, cur + ja - 1, cur + nc))
        loaded = [(slot, sc_ref[slot]) for slot in slots]
        for slot, s in loaded:
            eq = s == tau
            eqf = jnp.where(eq, 1.0, 0.0)
            later = jnp.dot(upper, eqf.astype(bf16), preferred_element_type=f32) + carry
            sel = jnp.logical_or(s > tau, jnp.logical_and(eq, later >= excess))
            sc_ref[slot] = jnp.where(sel, 0.0, NEG)
            carry = carry + jnp.sum(fold(eqf, jnp.sum), axis=0, keepdims=True)
        return carry

    lax.fori_loop(0, trips(i // 2 + 1), p3_body, jnp.zeros((1, TQ), f32))


def _out_kernel(x_ref, yc_ref, ya_ref, woc_ref, woa_ref, g_ref, o_ref):
    y = (jnp.dot(yc_ref[0], woc_ref[...], preferred_element_type=f32)
         + jnp.dot(ya_ref[0], woa_ref[...], preferred_element_type=f32))
    yn = y * lax.rsqrt(jnp.mean(y * y, axis=-1, keepdims=True) + EPS) * g_ref[...]
    o_ref[0] = x_ref[0] + yn


def _rope_tables(S):
    half = ROT_DIM // 2
    inv_freq = jnp.power(jnp.float32(ROPE_THETA), -jnp.arange(half, dtype=f32) / half)
    ang = jnp.arange(S, dtype=jnp.int32).astype(f32)[:, None] * inv_freq[None, :]
    cos, sin = jnp.cos(ang), jnp.sin(ang)
    ones = jnp.ones((S, HEAD_DIM - ROT_DIM), f32)
    zeros = jnp.zeros((S, HEAD_DIM - ROT_DIM), f32)
    zh = jnp.zeros((S, half), f32)
    c = jnp.concatenate([cos, cos, ones], axis=1)
    sa = jnp.concatenate([-sin, zh, zeros], axis=1)
    sb = jnp.concatenate([zh, sin, zeros], axis=1)
    rep = LANES // HEAD_DIM
    return jnp.tile(c, (1, rep)), jnp.tile(sa, (1, rep)), jnp.tile(sb, (1, rep))


def _const_spec(shape):
    nd = len(shape)
    return pl.BlockSpec(shape, lambda b, j: (0,) * nd, pipeline_mode=pl.Buffered(1))


def _layer(x, g_pre, w_in, conv_w, w_out, g_post):
    B, S, _ = x.shape
    assert S % T_PROJ == 0 and S % T_Q == 0 and S % T_OUT == 0 and T_Q == S_C == T_PROJ
    top_k = min(INDEX_TOPK, S // 4)
    nc = S // S_C

    wc = w_in[:, 0:4 * D_CONV].astype(bf16)
    wa = w_in[:, 4 * D_CONV:4 * D_CONV + 4 * D_ATTN].astype(bf16)
    o = 4 * D_CONV + 4 * D_ATTN
    nq = IDX_HEADS * IDX_DIM
    wi = jnp.zeros((D_MODEL, IDX_W), f32)
    wi = wi.at[:, 0:nq + IDX_DIM].set(w_in[:, o:o + nq + IDX_DIM])
    wi = wi.at[:, 3 * LANES:3 * LANES + IDX_HEADS].set(w_in[:, o + nq + IDX_DIM:o + nq + IDX_DIM + IDX_HEADS])
    wi = wi.astype(bf16)
    woc = w_out[0:D_CONV].astype(bf16)
    woa = w_out[D_CONV:].astype(bf16)
    c_t, sa_t, sb_t = _rope_tables(S)

    cp = functools.partial(pltpu.CompilerParams, vmem_limit_bytes=VMEM_LIMIT)

    tile = lambda w: pl.BlockSpec((1, T_PROJ, w), lambda b, j: (b, j, 0))
    ttile = lambda *shape: pl.BlockSpec((1, 1) + shape, lambda b, j: (b, j) + (0,) * len(shape))
    tab = pl.BlockSpec((T_PROJ, LANES), lambda b, j: (j, 0))
    sds = jax.ShapeDtypeStruct
    yc, qtz, k, vta, gz, qit, ki, sgt = pl.pallas_call(
        _proj_kernel,
        grid=(B, nc),
        in_specs=[tile(D_MODEL), _const_spec((1, D_MODEL)), _const_spec((D_MODEL, 4 * D_CONV)),
                  _const_spec((D_MODEL, 4 * D_ATTN)), _const_spec((D_MODEL, IDX_W)),
                  _const_spec((CONV_WIDTH, D_CONV)), tab, tab, tab],
        out_specs=[tile(D_CONV), ttile(N_HEADS, LANES, T_PROJ),
                   pl.BlockSpec((1, D_ATTN // LANES, T_PROJ, LANES), lambda b, j: (b, 0, j, 0)),
                   ttile(N_HEADS, V_ROWS, T_PROJ),
                   tile(D_ATTN), ttile(nq, T_PROJ), tile(IDX_DIM), ttile(SUBLANES, T_PROJ)],
        out_shape=[sds((B, S, D_CONV), bf16), sds((B, nc, N_HEADS, LANES, T_PROJ), bf16),
                   sds((B, D_ATTN // LANES, S, LANES), bf16), sds((B, nc, N_HEADS, V_ROWS, T_PROJ), bf16),
                   sds((B, S, D_ATTN), f32), sds((B, nc, nq, T_PROJ), bf16),
                   sds((B, S, IDX_DIM), bf16), sds((B, nc, SUBLANES, T_PROJ), f32)],
        scratch_shapes=[pltpu.VMEM((T_PROJ + 8, D_CONV), f32)],
        compiler_params=cp(dimension_semantics=("arbitrary", "arbitrary")),
        name="proj",
    )(x, g_pre.reshape(1, D_MODEL), wc, wa, wi, conv_w, c_t, sa_t, sb_t)

    nq_tiles = S // T_Q
    lag = lambda g: jnp.maximum(g - 1, 0)
    sel = lambda g: jnp.minimum(g, nq_tiles - 1)
    whole = lambda *shape: pl.BlockSpec((1,) + shape, lambda b, g: (b,) + (0,) * len(shape),
                                        pipeline_mode=pl.Buffered(1))
    ya = pl.pallas_call(
        functools.partial(_attn_kernel, top_k=top_k, n_tiles=nq_tiles),
        grid=(B, nq_tiles + 1),
        in_specs=[pl.BlockSpec((1, 1, N_HEADS, LANES, T_Q), lambda b, g: (b, lag(g), 0, 0, 0)),
                  pl.BlockSpec((1, 1, nq, T_Q), lambda b, g: (b, sel(g), 0, 0)),
                  pl.BlockSpec((1, 1, SUBLANES, T_Q), lambda b, g: (b, sel(g), 0, 0)),
                  pl.BlockSpec((1, T_Q, D_ATTN), lambda b, g: (b, lag(g), 0)),
                  whole(D_ATTN // LANES, S, LANES), whole(nc, N_HEADS, V_ROWS, S_C), whole(S, IDX_DIM)],
        out_specs=pl.BlockSpec((1, T_Q, D_ATTN), lambda b, g: (b, lag(g), 0)),
        out_shape=sds((B, S, D_ATTN), bf16),
        scratch_shapes=[pltpu.VMEM((2 * (nc + 1), S_C, T_Q), f32), pltpu.VMEM((N_HEADS, HEAD_DIM, T_Q), f32),
                        pltpu.VMEM((N_HEADS, SUBLANES, T_Q), f32), pltpu.VMEM((N_HEADS, SUBLANES, T_Q), f32),
                        pltpu.VMEM((N_HEADS, S_C, T_Q), bf16), pltpu.VMEM((N_HEADS, S_C, T_Q), bf16)],
        compiler_params=cp(dimension_semantics=("arbitrary", "arbitrary")),
        name="attn",
    )(qtz, qit, sgt, gz, k, vta, ki)

    otile = lambda w: pl.BlockSpec((1, T_OUT, w), lambda b, j: (b, j, 0))
    return pl.pallas_call(
        _out_kernel,
        grid=(B, S // T_OUT),
        in_specs=[otile(D_MODEL), otile(D_CONV), otile(D_ATTN), _const_spec((D_CONV, D_MODEL)),
                  _const_spec((D_ATTN, D_MODEL)), _const_spec((1, D_MODEL))],
        out_specs=otile(D_MODEL),
        out_shape=sds((B, S, D_MODEL), f32),
        compiler_params=cp(dimension_semantics=("arbitrary", "arbitrary")),
        name="outproj",
    )(x, yc, ya, woc, woa, g_post.reshape(1, D_MODEL))


def kernel(x, norm_pre_g, w_in, conv_w, w_out, norm_post_g):
    for l in range(norm_pre_g.shape[0]):
        x = _layer(x, norm_pre_g[l], w_in[l], conv_w[l], w_out[l], norm_post_g[l])
    return x
```

```python
import functools

import jax
import jax.numpy as jnp
from jax import lax
from jax.experimental import pallas as pl
from jax.experimental.pallas import tpu as pltpu

D_MODEL = 1024
CHUNK = 64
D_CONV = 512
CONV_WIDTH = 3
N_HEADS = 8
HEAD_DIM = 64
D_ATTN = N_HEADS * HEAD_DIM
ROT_DIM = HEAD_DIM // 4
ROPE_THETA = 500000.0
IDX_HEADS = 4
IDX_DIM = 64
INDEX_TOPK = 256
EPS = 1e-6

LANES = 128
SUBLANES = 8
T_PROJ = 256
T_Q = 256
S_C = 256
T_OUT = 512
IDX_W = 512
N_BISECT = 24
NEG = -1e30
RIDE = 16
V_ROWS = HEAD_DIM + 16
Q_SCALE = HEAD_DIM ** -0.5 * 1.4426950408889634
VMEM_LIMIT = 56 * 1024 * 1024

f32 = jnp.float32
bf16 = jnp.bfloat16


def _rope128(xb, c, sa, sb):
    return xb * c + pltpu.roll(xb, LANES - ROT_DIM // 2, 1) * sa + pltpu.roll(xb, ROT_DIM // 2, 1) * sb


def _proj_kernel(x_ref, g_ref, wc_ref, wa_ref, wi_ref, cw_ref, c_ref, sa_ref, sb_ref,
                 yc_ref, qtz_ref, k_ref, vta_ref, gz_ref, qit_ref, ki_ref, sgt_ref, ubuf):
    j = pl.program_id(1)
    T = T_PROJ

    @pl.when(j == 0)
    def _():
        ubuf[0:8, :] = jnp.zeros((8, D_CONV), f32)

    x = x_ref[0]
    xn = x * lax.rsqrt(jnp.mean(x * x, axis=-1, keepdims=True) + EPS) * g_ref[...]
    xn = xn.astype(bf16)

    def proj(w_ref, k):
        return jnp.dot(xn, w_ref[:, k * 512:(k + 1) * 512], preferred_element_type=f32)

    u = proj(wc_ref, 2) * proj(wc_ref, 0)
    ubuf[8:8 + T, :] = u
    conv = (ubuf[6:6 + T, :] * cw_ref[0:1, :] + ubuf[7:7 + T, :] * cw_ref[1:2, :] + u * cw_ref[2:3, :])
    ubuf[0:8, :] = ubuf[T:T + 8, :]
    zc = proj(wc_ref, 3)
    yc_ref[0] = (proj(wc_ref, 1) * conv * (zc * jax.nn.sigmoid(zc))).astype(bf16)

    c = c_ref[...]
    sa = sa_ref[...]
    sb = sb_ref[...]

    qf = proj(wa_ref, 0)
    kf = proj(wa_ref, 1)
    vf = proj(wa_ref, 2)
    sub = lax.broadcasted_iota(jnp.int32, (LANES, T), 0)
    for i in range(D_ATTN // LANES):
        sl = slice(i * LANES, (i + 1) * LANES)
        qt = (_rope128(qf[:, sl], c, sa, sb) * Q_SCALE).T
        qtz_ref[0, 0, 2 * i] = jnp.where(sub < HEAD_DIM, qt, 0.0).astype(bf16)
        qtz_ref[0, 0, 2 * i + 1] = jnp.where(sub >= HEAD_DIM, qt, 0.0).astype(bf16)
        k_ref[0, i] = _rope128(kf[:, sl], c, sa, sb).astype(bf16)
        vt = vf[:, sl].T.astype(bf16)
        for h in (2 * i, 2 * i + 1):
            vta_ref[0, 0, h, 0:HEAD_DIM, :] = vt[(h % 2) * HEAD_DIM:(h % 2 + 1) * HEAD_DIM, :]
            vta_ref[0, 0, h, HEAD_DIM:V_ROWS, :] = jnp.ones((V_ROWS - HEAD_DIM, T), bf16)
    za = proj(wa_ref, 3)
    gz_ref[0] = za * jax.nn.sigmoid(za)

    pi = jnp.dot(xn, wi_ref[...], preferred_element_type=f32)
    wv = pi[:, 3 * LANES:4 * LANES]
    sgn = jnp.where(wv > 0, 1.0, jnp.where(wv < 0, -1.0, 0.0))
    sgt_ref[0, 0] = sgn.T[0:SUBLANES, :]
    wabs = jnp.abs(wv) * ((IDX_DIM ** -0.5) * (IDX_HEADS ** -0.5))
    lane = lax.broadcasted_iota(jnp.int32, (T, LANES), 1)
    for i in range(2):
        sl = slice(i * LANES, (i + 1) * LANES)
        scale = jnp.where(lane < IDX_DIM, wabs[:, 2 * i:2 * i + 1], wabs[:, 2 * i + 1:2 * i + 2])
        qit_ref[0, 0, sl, :] = (_rope128(pi[:, sl], c, sa, sb) * scale).T.astype(bf16)
    ki_ref[0] = _rope128(pi[:, 2 * LANES:3 * LANES], c, sa, sb)[:, 0:IDX_DIM].astype(bf16)


def _attn_kernel(qtz_ref, qit_ref, sgt_ref, gz_ref, k_ref, vta_ref, ki_ref, o_ref,
                 sc_ref, acc_ref, m_ref, l_ref, a0_ref, a1_ref, *, top_k, n_tiles):
    g = pl.program_id(1)
    TQ, SC = T_Q, S_C
    G = SC // SUBLANES
    nslot = sc_ref.shape[0] // 2
    nc = nslot - 1
    i = jnp.minimum(g, n_tiles - 1)
    ip = g - 1
    live = g < n_tiles
    cur = (g % 2) * nslot
    prev = nslot - cur

    def fold(a, op):
        return op(a.reshape(G, SUBLANES, TQ), axis=0)

    def rows(j):
        return pl.ds(pl.multiple_of(j * SC, SC), SC)

    def trips(n):
        return jnp.where(live, n, 0)

    t_loc = lax.broadcasted_iota(jnp.int32, (1, TQ), 1)
    lim_local = ((t_loc // CHUNK) + 1) * CHUNK
    limit = lim_local + i * TQ
    keff = jnp.minimum(limit, top_k).astype(f32)
    adm_diag = lax.broadcasted_iota(jnp.int32, (SC, TQ), 0) < lim_local

    sgt = sgt_ref[0, 0]
    sg = [sgt[h:h + 1, :] for h in range(IDX_HEADS)]

    def score_chunk(j):
        kc = ki_ref[0, rows(j), :]
        s = None
        for h in range(IDX_HEADS):
            l = jnp.dot(kc, qit_ref[0, 0, h * IDX_DIM:(h + 1) * IDX_DIM, :], preferred_element_type=f32)
            t = jnp.maximum(l, 0.0) * sg[h]
            s = t if s is None else s + t
        return s

    minus_inf = jnp.full((SC, TQ), -jnp.inf, f32)
    sc_ref[cur + nc] = minus_inf
    sc_ref[prev + nc] = minus_inf

    def p1_chunks(js, carry):
        mn, mx = carry
        for j in js:
            s = score_chunk(j)
            sc_ref[cur + j] = s
            mn, mx = jnp.minimum(mn, fold(s, jnp.min)), jnp.maximum(mx, fold(s, jnp.max))
        return mn, mx

    mn0 = jnp.full((SUBLANES, TQ), jnp.inf, f32)
    mx0 = jnp.full((SUBLANES, TQ), -jnp.inf, f32)
    U = 4
    carry = lax.fori_loop(0, trips(i // U), lambda jj, c: p1_chunks([U * jj + q for q in range(U)], c), (mn0, mx0))
    mn, mx = lax.fori_loop(trips(i // U * U), trips(i), lambda j, c: p1_chunks([j], c), carry)
    s = score_chunk(i)
    sc_ref[cur + i] = jnp.where(adm_diag, s, -jnp.inf)
    mn = jnp.minimum(mn, fold(jnp.where(adm_diag, s, jnp.inf), jnp.min))
    mx = jnp.maximum(mx, fold(jnp.where(adm_diag, s, -jnp.inf), jnp.max))
    lo = jnp.min(mn, axis=0, keepdims=True)
    hi = jnp.max(mx, axis=0, keepdims=True)

    acc_ref[...] = jnp.zeros(acc_ref.shape, f32)
    m_ref[...] = jnp.full(m_ref.shape, NEG, f32)
    l_ref[...] = jnp.zeros(l_ref.shape, f32)
    P = 2 * SUBLANES

    def all_sublanes(x8, op):
        for shift in (4, 2, 1):
            x8 = op(x8, pltpu.roll(x8, shift, 0))
        return x8

    def kv_chunk(c):
        return jnp.clip(c, 0, jnp.maximum(ip, 0))

    def logits_to(buf, c, between=lambda h: None):
        bias = sc_ref[jnp.where(c <= ip, prev + c, prev + nc)]
        for h in range(N_HEADS):
            kc = k_ref[0, h // 2, rows(kv_chunk(c)), :]
            buf[h] = (jnp.dot(kc, qtz_ref[0, 0, h], preferred_element_type=f32) + bias).astype(bf16)
            between(h)

    def softmax_pv(buf, c, between=lambda h: None):
        for h in range(N_HEADS):
            a = buf[h]
            cmax = jnp.max(a.reshape(SC // P, P, TQ), axis=0).astype(f32)
            m_old = m_ref[h]
            m_new = jnp.maximum(m_old, all_sublanes(jnp.maximum(cmax[0:SUBLANES], cmax[SUBLANES:P]), jnp.maximum))
            alpha = jnp.exp2(m_old - m_new)
            m_ref[h] = m_new
            mb = jnp.concatenate([m_new, m_new], axis=0).astype(bf16)
            p = jnp.exp2(a.reshape(SC // P, P, TQ) - mb[None]).reshape(SC, TQ)
            pv = jnp.dot(vta_ref[0, kv_chunk(c), h], p, preferred_element_type=f32)
            l_ref[h] = alpha * l_ref[h] + pv[HEAD_DIM:HEAD_DIM + SUBLANES]
            acc = acc_ref[h].reshape(HEAD_DIM // SUBLANES, SUBLANES, TQ) * alpha[None]
            acc_ref[h] = acc.reshape(HEAD_DIM, TQ) + pv[0:HEAD_DIM]
            between(h)

    def chunk_count(slot, mid8):
        return jnp.sum(jnp.where(sc_ref[slot].reshape(G, SUBLANES, TQ) >= mid8[None], 1.0, 0.0), axis=0)

    def update(state, mid, c, take=True):
        lo, hi, clo = state
        ge = c >= keff
        up = jnp.logical_and(ge, take)
        down = jnp.logical_and(jnp.logical_not(ge), take)
        return jnp.where(up, mid, lo), jnp.where(down, mid, hi), jnp.where(up, c, clo)

    n = i + 1
    n_trips = (g + 1) // 2
    per_step = (n + RIDE - 1) // RIDE
    n_ride = jnp.where(live, jnp.minimum(n_trips // per_step, N_BISECT), 0)
    logits_to(a0_ref, 0)

    def p4_body(jj, carry):
        lo, hi, clo, cnt, cursor, done = carry
        j = 2 * jj
        active = done < n_ride
        mid = 0.5 * lo + 0.5 * hi
        mid8 = jnp.broadcast_to(mid, (SUBLANES, TQ))

        counts = [cnt]

        def ride(stage):
            per_stage = RIDE // 4

            def between(h):
                for t in range(h * per_stage // N_HEADS, (h + 1) * per_stage // N_HEADS):
                    c = cursor + stage * per_stage + t
                    slot = jnp.where(jnp.logical_and(c <= i, active), cur + c, cur + nc)
                    counts[0] = counts[0] + chunk_count(slot, mid8)
            return between

        logits_to(a1_ref, j + 1, ride(0))
        softmax_pv(a0_ref, j, ride(1))
        logits_to(a0_ref, j + 2, ride(2))
        softmax_pv(a1_ref, j + 1, ride(3))
        cnt = counts[0]
        cursor = cursor + RIDE
        full = jnp.logical_and(cursor >= n, active)
        lo, hi, clo = update((lo, hi, clo), mid, jnp.sum(cnt, axis=0, keepdims=True), take=full)
        keep = jnp.where(full, 0.0, 1.0)
        return lo, hi, clo, cnt * keep, jnp.where(full, 0, cursor), done + full.astype(jnp.int32)

    clo = limit.astype(f32)
    lo, hi, clo, _, _, done = lax.fori_loop(
        0, n_trips, p4_body, (lo, hi, clo, jnp.zeros((SUBLANES, TQ), f32), jnp.int32(0), jnp.int32(0)))

    @pl.when(g >= 1)
    def _():
        for hp in range(N_HEADS // 2):
            pair = jnp.concatenate(
                [(acc_ref[h].reshape(HEAD_DIM // SUBLANES, SUBLANES, TQ) / l_ref[h][None]).reshape(HEAD_DIM, TQ)
                 for h in (2 * hp, 2 * hp + 1)], axis=0)
            sl = slice(hp * LANES, (hp + 1) * LANES)
            o_ref[0, :, sl] = (pair.T * gz_ref[0, :, sl]).astype(bf16)

    def count_ge(mid):
        mid8 = jnp.broadcast_to(mid, (SUBLANES, TQ))

        def body(jj, cnt):
            j2 = 2 * jj + 1
            return (cnt + chunk_count(cur + 2 * jj, mid8)
                    + chunk_count(jnp.where(j2 <= i, cur + j2, cur + nc), mid8))
        cnt = lax.fori_loop(0, i // 2 + 1, body, jnp.zeros((SUBLANES, TQ), f32))
        return jnp.sum(cnt, axis=0, keepdims=True)

    def bisect(state):
        mid = 0.5 * state[0] + 0.5 * state[1]
        return update(state, mid, count_ge(mid))

    def snap(lo, hi):
        def body(jj, carry):
            a, b = carry
            for j in (2 * jj, jnp.minimum(2 * jj + 1, i)):
                s = sc_ref[cur + j]
                a = jnp.minimum(a, fold(jnp.where(s >= lo, s, jnp.inf), jnp.min))
                b = jnp.maximum(b, fold(jnp.where(s <= hi, s, -jnp.inf), jnp.max))
            return a, b
        a, b = lax.fori_loop(0, trips(i // 2 + 1), body, (mn0, mx0))
        return jnp.min(a, axis=0, keepdims=True), jnp.max(b, axis=0, keepdims=True)

    lo, hi, clo = lax.fori_loop(0, trips(N_BISECT - done), lambda _, st: bisect(st), (lo, hi, clo))
    vlo, vhi = snap(lo, hi)

    def unconverged(lo, hi, clo, vlo, vhi):
        bad = jnp.logical_and(clo != keff, vlo != vhi)
        return jnp.logical_and(live, jnp.max(jnp.where(bad, 1.0, 0.0)) > 0.0)

    def refine(state):
        lo, hi, clo, _, _ = state
        lo, hi, clo = bisect((lo, hi, clo))
        vlo, vhi = snap(lo, hi)
        return lo, hi, clo, vlo, vhi

    lo, hi, clo, vlo, vhi = lax.while_loop(lambda st: unconverged(*st), refine, (lo, hi, clo, vlo, vhi))
    tau = vlo
    excess = clo - keff

    ka = lax.broadcasted_iota(jnp.int32, (SC, SC), 0)
    kb = lax.broadcasted_iota(jnp.int32, (SC, SC), 1)
    upper = jnp.where(kb > ka, 1.0, 0.0).astype(bf16)

    def p3_chunks(slots, carry):
        loaded = [(slot, sc_ref[slot]) for slot in slots]
        for slot, s in loaded:
            eq = s == tau
            eqf = jnp.where(eq, 1.0, 0.0)
            later = jnp.dot(upper, eqf.astype(bf16), preferred_element_type=f32) + carry
            sel = jnp.logical_or(s > tau, jnp.logical_and(eq, later >= excess))
            sc_ref[slot] = jnp.where(sel, 0.0, NEG)
            carry = carry + jnp.sum(fold(eqf, jnp.sum), axis=0, keepdims=True)
        return carry

    rest = n % U
    carry = lax.fori_loop(0, trips(rest), lambda r, c: p3_chunks([cur + i - r], c), jnp.zeros((1, TQ), f32))
    lax.fori_loop(0, trips(n // U),
                  lambda jj, c: p3_chunks([cur + i - rest - U * jj - q for q in range(U)], c), carry)


def _out_kernel(x_ref, yc_ref, ya_ref, woc_ref, woa_ref, g_ref, o_ref):
    y = (jnp.dot(yc_ref[0], woc_ref[...], preferred_element_type=f32)
         + jnp.dot(ya_ref[0], woa_ref[...], preferred_element_type=f32))
    yn = y * lax.rsqrt(jnp.mean(y * y, axis=-1, keepdims=True) + EPS) * g_ref[...]
    o_ref[0] = x_ref[0] + yn


def _rope_tables(S):
    half = ROT_DIM // 2
    inv_freq = jnp.power(jnp.float32(ROPE_THETA), -jnp.arange(half, dtype=f32) / half)
    ang = jnp.arange(S, dtype=jnp.int32).astype(f32)[:, None] * inv_freq[None, :]
    cos, sin = jnp.cos(ang), jnp.sin(ang)
    ones = jnp.ones((S, HEAD_DIM - ROT_DIM), f32)
    zeros = jnp.zeros((S, HEAD_DIM - ROT_DIM), f32)
    zh = jnp.zeros((S, half), f32)
    c = jnp.concatenate([cos, cos, ones], axis=1)
    sa = jnp.concatenate([-sin, zh, zeros], axis=1)
    sb = jnp.concatenate([zh, sin, zeros], axis=1)
    rep = LANES // HEAD_DIM
    return jnp.tile(c, (1, rep)), jnp.tile(sa, (1, rep)), jnp.tile(sb, (1, rep))


def _const_spec(shape):
    nd = len(shape)
    return pl.BlockSpec(shape, lambda b, j: (0,) * nd, pipeline_mode=pl.Buffered(1))


def _layer(x, g_pre, w_in, conv_w, w_out, g_post):
    B, S, _ = x.shape
    assert S % T_PROJ == 0 and S % T_Q == 0 and S % T_OUT == 0 and T_Q == S_C == T_PROJ
    top_k = min(INDEX_TOPK, S // 4)
    nc = S // S_C

    wc = w_in[:, 0:4 * D_CONV].astype(bf16)
    wa = w_in[:, 4 * D_CONV:4 * D_CONV + 4 * D_ATTN].astype(bf16)
    o = 4 * D_CONV + 4 * D_ATTN
    nq = IDX_HEADS * IDX_DIM
    wi = jnp.zeros((D_MODEL, IDX_W), f32)
    wi = wi.at[:, 0:nq + IDX_DIM].set(w_in[:, o:o + nq + IDX_DIM])
    wi = wi.at[:, 3 * LANES:3 * LANES + IDX_HEADS].set(w_in[:, o + nq + IDX_DIM:o + nq + IDX_DIM + IDX_HEADS])
    wi = wi.astype(bf16)
    woc = w_out[0:D_CONV].astype(bf16)
    woa = w_out[D_CONV:].astype(bf16)
    c_t, sa_t, sb_t = _rope_tables(S)

    cp = functools.partial(pltpu.CompilerParams, vmem_limit_bytes=VMEM_LIMIT)

    tile = lambda w: pl.BlockSpec((1, T_PROJ, w), lambda b, j: (b, j, 0))
    ttile = lambda *shape: pl.BlockSpec((1, 1) + shape, lambda b, j: (b, j) + (0,) * len(shape))
    tab = pl.BlockSpec((T_PROJ, LANES), lambda b, j: (j, 0))
    sds = jax.ShapeDtypeStruct
    yc, qtz, k, vta, gz, qit, ki, sgt = pl.pallas_call(
        _proj_kernel,
        grid=(B, nc),
        in_specs=[tile(D_MODEL), _const_spec((1, D_MODEL)), _const_spec((D_MODEL, 4 * D_CONV)),
                  _const_spec((D_MODEL, 4 * D_ATTN)), _const_spec((D_MODEL, IDX_W)),
                  _const_spec((CONV_WIDTH, D_CONV)), tab, tab, tab],
        out_specs=[tile(D_CONV), ttile(N_HEADS, LANES, T_PROJ),
                   pl.BlockSpec((1, D_ATTN // LANES, T_PROJ, LANES), lambda b, j: (b, 0, j, 0)),
                   ttile(N_HEADS, V_ROWS, T_PROJ),
                   tile(D_ATTN), ttile(nq, T_PROJ), tile(IDX_DIM), ttile(SUBLANES, T_PROJ)],
        out_shape=[sds((B, S, D_CONV), bf16), sds((B, nc, N_HEADS, LANES, T_PROJ), bf16),
                   sds((B, D_ATTN // LANES, S, LANES), bf16), sds((B, nc, N_HEADS, V_ROWS, T_PROJ), bf16),
                   sds((B, S, D_ATTN), f32), sds((B, nc, nq, T_PROJ), bf16),
                   sds((B, S, IDX_DIM), bf16), sds((B, nc, SUBLANES, T_PROJ), f32)],
        scratch_shapes=[pltpu.VMEM((T_PROJ + 8, D_CONV), f32)],
        compiler_params=cp(dimension_semantics=("arbitrary", "arbitrary")),
        name="proj",
    )(x, g_pre.reshape(1, D_MODEL), wc, wa, wi, conv_w, c_t, sa_t, sb_t)

    nq_tiles = S // T_Q
    lag = lambda g: jnp.maximum(g - 1, 0)
    sel = lambda g: jnp.minimum(g, nq_tiles - 1)
    whole = lambda *shape: pl.BlockSpec((1,) + shape, lambda b, g: (b,) + (0,) * len(shape),
                                        pipeline_mode=pl.Buffered(1))
    ya = pl.pallas_call(
        functools.partial(_attn_kernel, top_k=top_k, n_tiles=nq_tiles),
        grid=(B, nq_tiles + 1),
        in_specs=[pl.BlockSpec((1, 1, N_HEADS, LANES, T_Q), lambda b, g: (b, lag(g), 0, 0, 0)),
                  pl.BlockSpec((1, 1, nq, T_Q), lambda b, g: (b, sel(g), 0, 0)),
                  pl.BlockSpec((1, 1, SUBLANES, T_Q), lambda b, g: (b, sel(g), 0, 0)),
                  pl.BlockSpec((1, T_Q, D_ATTN), lambda b, g: (b, lag(g), 0)),
                  whole(D_ATTN // LANES, S, LANES), whole(nc, N_HEADS, V_ROWS, S_C), whole(S, IDX_DIM)],
        out_specs=pl.BlockSpec((1, T_Q, D_ATTN), lambda b, g: (b, lag(g), 0)),
        out_shape=sds((B, S, D_ATTN), bf16),
        scratch_shapes=[pltpu.VMEM((2 * (nc + 1), S_C, T_Q), f32), pltpu.VMEM((N_HEADS, HEAD_DIM, T_Q), f32),
                        pltpu.VMEM((N_HEADS, SUBLANES, T_Q), f32), pltpu.VMEM((N_HEADS, SUBLANES, T_Q), f32),
                        pltpu.VMEM((N_HEADS, S_C, T_Q), bf16), pltpu.VMEM((N_HEADS, S_C, T_Q), bf16)],
        compiler_params=cp(dimension_semantics=("arbitrary", "arbitrary")),
        name="attn",
    )(qtz, qit, sgt, gz, k, vta, ki)

    otile = lambda w: pl.BlockSpec((1, T_OUT, w), lambda b, j: (b, j, 0))
    return pl.pallas_call(
        _out_kernel,
        grid=(B, S // T_OUT),
        in_specs=[otile(D_MODEL), otile(D_CONV), otile(D_ATTN), _const_spec((D_CONV, D_MODEL)),
                  _const_spec((D_ATTN, D_MODEL)), _const_spec((1, D_MODEL))],
        out_specs=otile(D_MODEL),
        out_shape=sds((B, S, D_MODEL), f32),
        compiler_params=cp(dimension_semantics=("arbitrary", "arbitrary")),
        name="outproj",
    )(x, yc, ya, woc, woa, g_post.reshape(1, D_MODEL))


def kernel(x, norm_pre_g, w_in, conv_w, w_out, norm_post_g):
    for l in range(norm_pre_g.shape[0]):
        x = _layer(x, norm_pre_g[l], w_in[l], conv_w[l], w_out[l], norm_post_g[l])
    return x
```

```python
import functools

import jax
import jax.numpy as jnp
from jax import lax
from jax.experimental import pallas as pl
from jax.experimental.pallas import tpu as pltpu

D_MODEL = 1024
CHUNK = 64
D_CONV = 512
CONV_WIDTH = 3
N_HEADS = 8
HEAD_DIM = 64
D_ATTN = N_HEADS * HEAD_DIM
ROT_DIM = HEAD_DIM // 4
ROPE_THETA = 500000.0
IDX_HEADS = 4
IDX_DIM = 64
INDEX_TOPK = 256
EPS = 1e-6

LANES = 128
SUBLANES = 8
T_PROJ = 256
T_Q = 256
S_C = 256
T_OUT = 512
IDX_W = 512
N_BISECT = 24
NEG = -1e30
RIDE = 16
N_FULL = 16
V_ROWS = HEAD_DIM + 16
Q_SCALE = HEAD_DIM ** -0.5 * 1.4426950408889634
VMEM_LIMIT = 56 * 1024 * 1024

f32 = jnp.float32
bf16 = jnp.bfloat16


def _rope128(xb, c, sa, sb):
    return xb * c + pltpu.roll(xb, LANES - ROT_DIM // 2, 1) * sa + pltpu.roll(xb, ROT_DIM // 2, 1) * sb


def _proj_kernel(x_ref, g_ref, wc_ref, wa_ref, wi_ref, cw_ref, c_ref, sa_ref, sb_ref,
                 yc_ref, qtz_ref, k_ref, vta_ref, gz_ref, qit_ref, ki_ref, sgt_ref, ubuf):
    j = pl.program_id(1)
    T = T_PROJ

    @pl.when(j == 0)
    def _():
        ubuf[0:8, :] = jnp.zeros((8, D_CONV), f32)

    x = x_ref[0]
    xn = x * lax.rsqrt(jnp.mean(x * x, axis=-1, keepdims=True) + EPS) * g_ref[...]
    xn = xn.astype(bf16)

    def proj(w_ref, k):
        return jnp.dot(xn, w_ref[:, k * 512:(k + 1) * 512], preferred_element_type=f32)

    u = proj(wc_ref, 2) * proj(wc_ref, 0)
    ubuf[8:8 + T, :] = u
    conv = (ubuf[6:6 + T, :] * cw_ref[0:1, :] + ubuf[7:7 + T, :] * cw_ref[1:2, :] + u * cw_ref[2:3, :])
    ubuf[0:8, :] = ubuf[T:T + 8, :]
    zc = proj(wc_ref, 3)
    yc_ref[0] = (proj(wc_ref, 1) * conv * (zc * jax.nn.sigmoid(zc))).astype(bf16)

    c = c_ref[...]
    sa = sa_ref[...]
    sb = sb_ref[...]

    qf = proj(wa_ref, 0)
    kf = proj(wa_ref, 1)
    vf = proj(wa_ref, 2)
    sub = lax.broadcasted_iota(jnp.int32, (LANES, T), 0)
    for i in range(D_ATTN // LANES):
        sl = slice(i * LANES, (i + 1) * LANES)
        qt = (_rope128(qf[:, sl], c, sa, sb) * Q_SCALE).T
        qtz_ref[0, 0, 2 * i] = jnp.where(sub < HEAD_DIM, qt, 0.0).astype(bf16)
        qtz_ref[0, 0, 2 * i + 1] = jnp.where(sub >= HEAD_DIM, qt, 0.0).astype(bf16)
        k_ref[0, i] = _rope128(kf[:, sl], c, sa, sb).astype(bf16)
        vt = vf[:, sl].T.astype(bf16)
        for h in (2 * i, 2 * i + 1):
            vta_ref[0, 0, h, 0:HEAD_DIM, :] = vt[(h % 2) * HEAD_DIM:(h % 2 + 1) * HEAD_DIM, :]
            vta_ref[0, 0, h, HEAD_DIM:V_ROWS, :] = jnp.ones((V_ROWS - HEAD_DIM, T), bf16)
    za = proj(wa_ref, 3)
    gz_ref[0] = za * jax.nn.sigmoid(za)

    pi = jnp.dot(xn, wi_ref[...], preferred_element_type=f32)
    wv = pi[:, 3 * LANES:4 * LANES]
    sgn = jnp.where(wv > 0, 1.0, jnp.where(wv < 0, -1.0, 0.0))
    sgt_ref[0, 0] = sgn.T[0:SUBLANES, :]
    wabs = jnp.abs(wv) * ((IDX_DIM ** -0.5) * (IDX_HEADS ** -0.5))
    lane = lax.broadcasted_iota(jnp.int32, (T, LANES), 1)
    for i in range(2):
        sl = slice(i * LANES, (i + 1) * LANES)
        scale = jnp.where(lane < IDX_DIM, wabs[:, 2 * i:2 * i + 1], wabs[:, 2 * i + 1:2 * i + 2])
        qit_ref[0, 0, sl, :] = (_rope128(pi[:, sl], c, sa, sb) * scale).T.astype(bf16)
    ki_ref[0] = _rope128(pi[:, 2 * LANES:3 * LANES], c, sa, sb)[:, 0:IDX_DIM].astype(bf16)


def _attn_kernel(qtz_ref, qit_ref, sgt_ref, gz_ref, k_ref, vta_ref, ki_ref, o_ref,
                 sc_ref, acc_ref, m_ref, l_ref, a0_ref, a1_ref, fm_ref, fc_ref, *, top_k, n_tiles):
    g = pl.program_id(1)
    TQ, SC = T_Q, S_C
    G = SC // SUBLANES
    nslot = sc_ref.shape[0] // 2
    nc = nslot - 1
    i = jnp.minimum(g, n_tiles - 1)
    ip = g - 1
    live = g < n_tiles
    cur = (g % 2) * nslot
    prev = nslot - cur

    def fold(a, op):
        return op(a.reshape(G, SUBLANES, TQ), axis=0)

    def rows(j):
        return pl.ds(pl.multiple_of(j * SC, SC), SC)

    def trips(n):
        return jnp.where(live, n, 0)

    t_loc = lax.broadcasted_iota(jnp.int32, (1, TQ), 1)
    lim_local = ((t_loc // CHUNK) + 1) * CHUNK
    limit = lim_local + i * TQ
    keff = jnp.minimum(limit, top_k).astype(f32)
    adm_diag = lax.broadcasted_iota(jnp.int32, (SC, TQ), 0) < lim_local

    sgt = sgt_ref[0, 0]
    sg = [sgt[h:h + 1, :] for h in range(IDX_HEADS)]

    def score_chunk(j):
        kc = ki_ref[0, rows(j), :]
        s = None
        for h in range(IDX_HEADS):
            l = jnp.dot(kc, qit_ref[0, 0, h * IDX_DIM:(h + 1) * IDX_DIM, :], preferred_element_type=f32)
            t = jnp.maximum(l, 0.0) * sg[h]
            s = t if s is None else s + t
        return s

    minus_inf = jnp.full((SC, TQ), -jnp.inf, f32)
    sc_ref[cur + nc] = minus_inf
    sc_ref[prev + nc] = minus_inf

    def p1_chunks(js, carry):
        mn, mx = carry
        for j in js:
            s = score_chunk(j)
            sc_ref[cur + j] = s
            mn, mx = jnp.minimum(mn, fold(s, jnp.min)), jnp.maximum(mx, fold(s, jnp.max))
        return mn, mx

    mn0 = jnp.full((SUBLANES, TQ), jnp.inf, f32)
    mx0 = jnp.full((SUBLANES, TQ), -jnp.inf, f32)
    U = 4
    carry = lax.fori_loop(0, trips(i // U), lambda jj, c: p1_chunks([U * jj + q for q in range(U)], c), (mn0, mx0))
    mn, mx = lax.fori_loop(trips(i // U * U), trips(i), lambda j, c: p1_chunks([j], c), carry)
    s = score_chunk(i)
    sc_ref[cur + i] = jnp.where(adm_diag, s, -jnp.inf)
    mn = jnp.minimum(mn, fold(jnp.where(adm_diag, s, jnp.inf), jnp.min))
    mx = jnp.maximum(mx, fold(jnp.where(adm_diag, s, -jnp.inf), jnp.max))
    lo = jnp.min(mn, axis=0, keepdims=True)
    hi = jnp.max(mx, axis=0, keepdims=True)

    acc_ref[...] = jnp.zeros(acc_ref.shape, f32)
    m_ref[...] = jnp.full(m_ref.shape, NEG, f32)
    l_ref[...] = jnp.zeros(l_ref.shape, f32)
    P = 2 * SUBLANES

    def all_sublanes(x8, op):
        for shift in (4, 2, 1):
            x8 = op(x8, pltpu.roll(x8, shift, 0))
        return x8

    def kv_chunk(c):
        return jnp.clip(c, 0, jnp.maximum(ip, 0))

    def logits_to(buf, c, between=lambda h: None):
        bias = sc_ref[jnp.where(c <= ip, prev + c, prev + nc)]
        for h in range(N_HEADS):
            kc = k_ref[0, h // 2, rows(kv_chunk(c)), :]
            buf[h] = (jnp.dot(kc, qtz_ref[0, 0, h], preferred_element_type=f32) + bias).astype(bf16)
            between(h)

    def softmax_pv(buf, c, between=lambda h: None):
        for h in range(N_HEADS):
            a = buf[h]
            cmax = jnp.max(a.reshape(SC // P, P, TQ), axis=0).astype(f32)
            m_old = m_ref[h]
            m_new = jnp.maximum(m_old, all_sublanes(jnp.maximum(cmax[0:SUBLANES], cmax[SUBLANES:P]), jnp.maximum))
            alpha = jnp.exp2(m_old - m_new)
            m_ref[h] = m_new
            mb = jnp.concatenate([m_new, m_new], axis=0).astype(bf16)
            p = jnp.exp2(a.reshape(SC // P, P, TQ) - mb[None]).reshape(SC, TQ)
            pv = jnp.dot(vta_ref[0, kv_chunk(c), h], p, preferred_element_type=f32)
            l_ref[h] = alpha * l_ref[h] + pv[HEAD_DIM:HEAD_DIM + SUBLANES]
            acc = acc_ref[h].reshape(HEAD_DIM // SUBLANES, SUBLANES, TQ) * alpha[None]
            acc_ref[h] = acc.reshape(HEAD_DIM, TQ) + pv[0:HEAD_DIM]
            between(h)

    def chunk_count(slot, mid8):
        return jnp.sum(jnp.where(sc_ref[slot].reshape(G, SUBLANES, TQ) >= mid8[None], 1.0, 0.0), axis=0)

    def update(state, mid, c, take=True):
        lo, hi, clo = state
        ge = c >= keff
        up = jnp.logical_and(ge, take)
        down = jnp.logical_and(jnp.logical_not(ge), take)
        return jnp.where(up, mid, lo), jnp.where(down, mid, hi), jnp.where(up, c, clo)

    n = i + 1
    n_trips = (g + 1) // 2
    per_step = (n + RIDE - 1) // RIDE
    n_ride = jnp.where(live, jnp.minimum(n_trips // per_step, N_BISECT), 0)
    logits_to(a0_ref, 0)

    def p4_body(jj, carry):
        lo, hi, clo, cnt, cursor, done = carry
        j = 2 * jj
        active = done < n_ride
        mid = 0.5 * lo + 0.5 * hi
        mid8 = jnp.broadcast_to(mid, (SUBLANES, TQ))

        counts = [cnt]

        def ride(stage):
            per_stage = RIDE // 4

            def between(h):
                for t in range(h * per_stage // N_HEADS, (h + 1) * per_stage // N_HEADS):
                    c = cursor + stage * per_stage + t
                    slot = jnp.where(jnp.logical_and(c <= i, active), cur + c, cur + nc)
                    counts[0] = counts[0] + chunk_count(slot, mid8)
            return between

        logits_to(a1_ref, j + 1, ride(0))
        softmax_pv(a0_ref, j, ride(1))
        logits_to(a0_ref, j + 2, ride(2))
        softmax_pv(a1_ref, j + 1, ride(3))
        cnt = counts[0]
        cursor = cursor + RIDE
        full = jnp.logical_and(cursor >= n, active)
        lo, hi, clo = update((lo, hi, clo), mid, jnp.sum(cnt, axis=0, keepdims=True), take=full)
        keep = jnp.where(full, 0.0, 1.0)
        return lo, hi, clo, cnt * keep, jnp.where(full, 0, cursor), done + full.astype(jnp.int32)

    clo = limit.astype(f32)
    lo, hi, clo, _, _, done = lax.fori_loop(
        0, n_trips, p4_body, (lo, hi, clo, jnp.zeros((SUBLANES, TQ), f32), jnp.int32(0), jnp.int32(0)))

    @pl.when(g >= 1)
    def _():
        for hp in range(N_HEADS // 2):
            pair = jnp.concatenate(
                [(acc_ref[h].reshape(HEAD_DIM // SUBLANES, SUBLANES, TQ) / l_ref[h][None]).reshape(HEAD_DIM, TQ)
                 for h in (2 * hp, 2 * hp + 1)], axis=0)
            sl = slice(hp * LANES, (hp + 1) * LANES)
            o_ref[0, :, sl] = (pair.T * gz_ref[0, :, sl]).astype(bf16)

    def count_ge(mid):
        mid8 = jnp.broadcast_to(mid, (SUBLANES, TQ))

        def body(jj, cnt):
            j2 = 2 * jj + 1
            return (cnt + chunk_count(cur + 2 * jj, mid8)
                    + chunk_count(jnp.where(j2 <= i, cur + j2, cur + nc), mid8))
        cnt = lax.fori_loop(0, i // 2 + 1, body, jnp.zeros((SUBLANES, TQ), f32))
        return jnp.sum(cnt, axis=0, keepdims=True)

    def bisect(state):
        mid = 0.5 * state[0] + 0.5 * state[1]
        return update(state, mid, count_ge(mid))

    def snap(lo, hi):
        def body(jj, carry):
            a, b = carry
            for j in (2 * jj, jnp.minimum(2 * jj + 1, i)):
                s = sc_ref[cur + j]
                a = jnp.minimum(a, fold(jnp.where(s >= lo, s, jnp.inf), jnp.min))
                b = jnp.maximum(b, fold(jnp.where(s <= hi, s, -jnp.inf), jnp.max))
            return a, b
        a, b = lax.fori_loop(0, trips(i // 2 + 1), body, (mn0, mx0))
        return jnp.min(a, axis=0, keepdims=True), jnp.max(b, axis=0, keepdims=True)

    lo, hi, clo = lax.fori_loop(0, trips(jnp.maximum(N_FULL - done, 0)), lambda _, st: bisect(st), (lo, hi, clo))

    fm_ref[nc] = jnp.full((SUBLANES, TQ), -jnp.inf, f32)
    fc_ref[nc] = jnp.zeros((SUBLANES, TQ), f32)

    def fold_body(jj, carry):
        tot, mixed = carry
        j2 = 2 * jj + 1
        for j, slot in ((2 * jj, cur + 2 * jj), (jnp.where(j2 <= i, j2, nc), jnp.where(j2 <= i, cur + j2, cur + nc))):
            s = sc_ref[slot]
            inside = jnp.logical_and(s >= lo, s < hi)
            vmax = fold(jnp.where(inside, s, -jnp.inf), jnp.max)
            vmin = fold(jnp.where(inside, s, jnp.inf), jnp.min)
            cnt = fold(jnp.where(inside, 1.0, 0.0), jnp.sum)
            fm_ref[j] = vmax
            fc_ref[j] = cnt
            tot = tot + cnt
            mixed = jnp.maximum(mixed, jnp.where(jnp.logical_and(cnt > 0.0, vmax != vmin), 1.0, 0.0))
        return tot, mixed

    zeros8 = jnp.zeros((SUBLANES, TQ), f32)
    tot, mixed = lax.fori_loop(0, trips(i // 2 + 1), fold_body, (zeros8, zeros8))
    above = clo - jnp.sum(tot, axis=0, keepdims=True)
    foldable = jnp.max(mixed) == 0.0

    def folded_count(mid):
        mid8 = jnp.broadcast_to(mid, (SUBLANES, TQ))

        def body(jj, cnt):
            for q in range(4):
                j = jnp.where(4 * jj + q <= i, 4 * jj + q, nc)
                cnt = cnt + jnp.where(fm_ref[j] >= mid8, fc_ref[j], 0.0)
            return cnt
        cnt = lax.fori_loop(0, i // 4 + 1, body, zeros8)
        return above + jnp.sum(cnt, axis=0, keepdims=True)

    def folded_bisect(state):
        mid = 0.5 * state[0] + 0.5 * state[1]
        return update(state, mid, folded_count(mid))

    n_rest = trips(N_BISECT - jnp.maximum(N_FULL, done))
    lo, hi, clo = lax.cond(
        foldable,
        lambda st: lax.fori_loop(0, n_rest, lambda _, t: folded_bisect(t), st),
        lambda st: lax.fori_loop(0, n_rest, lambda _, t: bisect(t), st),
        (lo, hi, clo))
    vlo, vhi = snap(lo, hi)

    def unconverged(lo, hi, clo, vlo, vhi):
        bad = jnp.logical_and(clo != keff, vlo != vhi)
        return jnp.logical_and(live, jnp.max(jnp.where(bad, 1.0, 0.0)) > 0.0)

    def refine(state):
        lo, hi, clo, _, _ = state
        lo, hi, clo = bisect((lo, hi, clo))
        vlo, vhi = snap(lo, hi)
        return lo, hi, clo, vlo, vhi

    lo, hi, clo, vlo, vhi = lax.while_loop(lambda st: unconverged(*st), refine, (lo, hi, clo, vlo, vhi))
    tau = vlo
    excess = clo - keff

    ka = lax.broadcasted_iota(jnp.int32, (SC, SC), 0)
    kb = lax.broadcasted_iota(jnp.int32, (SC, SC), 1)
    upper = jnp.where(kb > ka, 1.0, 0.0).astype(bf16)

    def p3_chunks(slots, carry):
        loaded = [(slot, sc_ref[slot]) for slot in slots]
        for slot, s in loaded:
            eq = s == tau
            eqf = jnp.where(eq, 1.0, 0.0)
            later = jnp.dot(upper, eqf.astype(bf16), preferred_element_type=f32) + carry
            sel = jnp.logical_or(s > tau, jnp.logical_and(eq, later >= excess))
            sc_ref[slot] = jnp.where(sel, 0.0, NEG)
            carry = carry + jnp.sum(fold(eqf, jnp.sum), axis=0, keepdims=True)
        return carry

    rest = n % U
    carry = lax.fori_loop(0, trips(rest), lambda r, c: p3_chunks([cur + i - r], c), jnp.zeros((1, TQ), f32))
    lax.fori_loop(0, trips(n // U),
                  lambda jj, c: p3_chunks([cur + i - rest - U * jj - q for q in range(U)], c), carry)


def _out_kernel(x_ref, yc_ref, ya_ref, woc_ref, woa_ref, g_ref, o_ref):
    y = (jnp.dot(yc_ref[0], woc_ref[...], preferred_element_type=f32)
         + jnp.dot(ya_ref[0], woa_ref[...], preferred_element_type=f32))
    yn = y * lax.rsqrt(jnp.mean(y * y, axis=-1, keepdims=True) + EPS) * g_ref[...]
    o_ref[0] = x_ref[0] + yn


def _rope_tables(S):
    half = ROT_DIM // 2
    inv_freq = jnp.power(jnp.float32(ROPE_THETA), -jnp.arange(half, dtype=f32) / half)
    ang = jnp.arange(S, dtype=jnp.int32).astype(f32)[:, None] * inv_freq[None, :]
    cos, sin = jnp.cos(ang), jnp.sin(ang)
    ones = jnp.ones((S, HEAD_DIM - ROT_DIM), f32)
    zeros = jnp.zeros((S, HEAD_DIM - ROT_DIM), f32)
    zh = jnp.zeros((S, half), f32)
    c = jnp.concatenate([cos, cos, ones], axis=1)
    sa = jnp.concatenate([-sin, zh, zeros], axis=1)
    sb = jnp.concatenate([zh, sin, zeros], axis=1)
    rep = LANES // HEAD_DIM
    return jnp.tile(c, (1, rep)), jnp.tile(sa, (1, rep)), jnp.tile(sb, (1, rep))


def _const_spec(shape):
    nd = len(shape)
    return pl.BlockSpec(shape, lambda b, j: (0,) * nd, pipeline_mode=pl.Buffered(1))


def _layer(x, g_pre, w_in, conv_w, w_out, g_post):
    B, S, _ = x.shape
    assert S % T_PROJ == 0 and S % T_Q == 0 and S % T_OUT == 0 and T_Q == S_C == T_PROJ
    top_k = min(INDEX_TOPK, S // 4)
    nc = S // S_C

    wc = w_in[:, 0:4 * D_CONV].astype(bf16)
    wa = w_in[:, 4 * D_CONV:4 * D_CONV + 4 * D_ATTN].astype(bf16)
    o = 4 * D_CONV + 4 * D_ATTN
    nq = IDX_HEADS * IDX_DIM
    wi = jnp.zeros((D_MODEL, IDX_W), f32)
    wi = wi.at[:, 0:nq + IDX_DIM].set(w_in[:, o:o + nq + IDX_DIM])
    wi = wi.at[:, 3 * LANES:3 * LANES + IDX_HEADS].set(w_in[:, o + nq + IDX_DIM:o + nq + IDX_DIM + IDX_HEADS])
    wi = wi.astype(bf16)
    woc = w_out[0:D_CONV].astype(bf16)
    woa = w_out[D_CONV:].astype(bf16)
    c_t, sa_t, sb_t = _rope_tables(S)

    cp = functools.partial(pltpu.CompilerParams, vmem_limit_bytes=VMEM_LIMIT)

    tile = lambda w: pl.BlockSpec((1, T_PROJ, w), lambda b, j: (b, j, 0))
    ttile = lambda *shape: pl.BlockSpec((1, 1) + shape, lambda b, j: (b, j) + (0,) * len(shape))
    tab = pl.BlockSpec((T_PROJ, LANES), lambda b, j: (j, 0))
    sds = jax.ShapeDtypeStruct
    yc, qtz, k, vta, gz, qit, ki, sgt = pl.pallas_call(
        _proj_kernel,
        grid=(B, nc),
        in_specs=[tile(D_MODEL), _const_spec((1, D_MODEL)), _const_spec((D_MODEL, 4 * D_CONV)),
                  _const_spec((D_MODEL, 4 * D_ATTN)), _const_spec((D_MODEL, IDX_W)),
                  _const_spec((CONV_WIDTH, D_CONV)), tab, tab, tab],
        out_specs=[tile(D_CONV), ttile(N_HEADS, LANES, T_PROJ),
                   pl.BlockSpec((1, D_ATTN // LANES, T_PROJ, LANES), lambda b, j: (b, 0, j, 0)),
                   ttile(N_HEADS, V_ROWS, T_PROJ),
                   tile(D_ATTN), ttile(nq, T_PROJ), tile(IDX_DIM), ttile(SUBLANES, T_PROJ)],
        out_shape=[sds((B, S, D_CONV), bf16), sds((B, nc, N_HEADS, LANES, T_PROJ), bf16),
                   sds((B, D_ATTN // LANES, S, LANES), bf16), sds((B, nc, N_HEADS, V_ROWS, T_PROJ), bf16),
                   sds((B, S, D_ATTN), f32), sds((B, nc, nq, T_PROJ), bf16),
                   sds((B, S, IDX_DIM), bf16), sds((B, nc, SUBLANES, T_PROJ), f32)],
        scratch_shapes=[pltpu.VMEM((T_PROJ + 8, D_CONV), f32)],
        compiler_params=cp(dimension_semantics=("arbitrary", "arbitrary")),
        name="proj",
    )(x, g_pre.reshape(1, D_MODEL), wc, wa, wi, conv_w, c_t, sa_t, sb_t)

    nq_tiles = S // T_Q
    lag = lambda g: jnp.maximum(g - 1, 0)
    sel = lambda g: jnp.minimum(g, nq_tiles - 1)
    whole = lambda *shape: pl.BlockSpec((1,) + shape, lambda b, g: (b,) + (0,) * len(shape),
                                        pipeline_mode=pl.Buffered(1))
    ya = pl.pallas_call(
        functools.partial(_attn_kernel, top_k=top_k, n_tiles=nq_tiles),
        grid=(B, nq_tiles + 1),
        in_specs=[pl.BlockSpec((1, 1, N_HEADS, LANES, T_Q), lambda b, g: (b, lag(g), 0, 0, 0)),
                  pl.BlockSpec((1, 1, nq, T_Q), lambda b, g: (b, sel(g), 0, 0)),
                  pl.BlockSpec((1, 1, SUBLANES, T_Q), lambda b, g: (b, sel(g), 0, 0)),
                  pl.BlockSpec((1, T_Q, D_ATTN), lambda b, g: (b, lag(g), 0)),
                  whole(D_ATTN // LANES, S, LANES), whole(nc, N_HEADS, V_ROWS, S_C), whole(S, IDX_DIM)],
        out_specs=pl.BlockSpec((1, T_Q, D_ATTN), lambda b, g: (b, lag(g), 0)),
        out_shape=sds((B, S, D_ATTN), bf16),
        scratch_shapes=[pltpu.VMEM((2 * (nc + 1), S_C, T_Q), f32), pltpu.VMEM((N_HEADS, HEAD_DIM, T_Q), f32),
                        pltpu.VMEM((N_HEADS, SUBLANES, T_Q), f32), pltpu.VMEM((N_HEADS, SUBLANES, T_Q), f32),
                        pltpu.VMEM((N_HEADS, S_C, T_Q), bf16), pltpu.VMEM((N_HEADS, S_C, T_Q), bf16),
                        pltpu.VMEM((nc + 1, SUBLANES, T_Q), f32), pltpu.VMEM((nc + 1, SUBLANES, T_Q), f32)],
        compiler_params=cp(dimension_semantics=("arbitrary", "arbitrary")),
        name="attn",
    )(qtz, qit, sgt, gz, k, vta, ki)

    otile = lambda w: pl.BlockSpec((1, T_OUT, w), lambda b, j: (b, j, 0))
    return pl.pallas_call(
        _out_kernel,
        grid=(B, S // T_OUT),
        in_specs=[otile(D_MODEL), otile(D_CONV), otile(D_ATTN), _const_spec((D_CONV, D_MODEL)),
                  _const_spec((D_ATTN, D_MODEL)), _const_spec((1, D_MODEL))],
        out_specs=otile(D_MODEL),
        out_shape=sds((B, S, D_MODEL), f32),
        compiler_params=cp(dimension_semantics=("arbitrary", "arbitrary")),
        name="outproj",
    )(x, yc, ya, woc, woa, g_post.reshape(1, D_MODEL))


def kernel(x, norm_pre_g, w_in, conv_w, w_out, norm_post_g):
    for l in range(norm_pre_g.shape[0]):
        x = _layer(x, norm_pre_g[l], w_in[l], conv_w[l], w_out[l], norm_post_g[l])
    return x
```

```python
import functools

import jax
import jax.numpy as jnp
from jax import lax
from jax.experimental import pallas as pl
from jax.experimental.pallas import tpu as pltpu

D_MODEL = 1024
CHUNK = 64
D_CONV = 512
CONV_WIDTH = 3
N_HEADS = 8
HEAD_DIM = 64
D_ATTN = N_HEADS * HEAD_DIM
ROT_DIM = HEAD_DIM // 4
ROPE_THETA = 500000.0
IDX_HEADS = 4
IDX_DIM = 64
INDEX_TOPK = 256
EPS = 1e-6

LANES = 128
SUBLANES = 8
T_PROJ = 512
T_Q = 256
S_C = 256
T_OUT = 1024
IDX_W = 512
N_BISECT = 24
NEG = -1e30
RIDE = 16
V_ROWS = HEAD_DIM + 16
Q_SCALE = HEAD_DIM ** -0.5 * 1.4426950408889634
VMEM_LIMIT = 56 * 1024 * 1024

f32 = jnp.float32
bf16 = jnp.bfloat16


def _rope128(xb, c, sa, sb):
    return xb * c + pltpu.roll(xb, LANES - ROT_DIM // 2, 1) * sa + pltpu.roll(xb, ROT_DIM // 2, 1) * sb


def _cast_kernel(w_ref, o_ref):
    o_ref[...] = w_ref[...].astype(bf16)


def _proj_kernel(x_ref, g_ref, wm_ref, wi_ref, cw_ref, c_ref, sa_ref, sb_ref,
                 yc_ref, qtz_ref, k_ref, vta_ref, gz_ref, qit_ref, ki_ref, sgt_ref, ubuf):
    j = pl.program_id(1)
    T = T_PROJ

    @pl.when(j == 0)
    def _():
        ubuf[0:8, :] = jnp.zeros((8, D_CONV), f32)

    x = x_ref[0]
    xn = x * lax.rsqrt(jnp.mean(x * x, axis=-1, keepdims=True) + EPS) * g_ref[...]
    xn = xn.astype(bf16)

    wc_ref, wa_ref = 0, 4

    def proj(first, k):
        return jnp.dot(xn, wm_ref[:, (first + k) * 512:(first + k + 1) * 512], preferred_element_type=f32)

    u = proj(wc_ref, 2) * proj(wc_ref, 0)
    ubuf[8:8 + T, :] = u
    conv = (ubuf[6:6 + T, :] * cw_ref[0:1, :] + ubuf[7:7 + T, :] * cw_ref[1:2, :] + u * cw_ref[2:3, :])
    ubuf[0:8, :] = ubuf[T:T + 8, :]
    zc = proj(wc_ref, 3)
    yc_ref[0] = (proj(wc_ref, 1) * conv * (zc * jax.nn.sigmoid(zc))).astype(bf16)

    c = c_ref[...]
    sa = sa_ref[...]
    sb = sb_ref[...]

    qf = proj(wa_ref, 0)
    kf = proj(wa_ref, 1)
    vf = proj(wa_ref, 2)
    sub = lax.broadcasted_iota(jnp.int32, (LANES, S_C), 0)
    chunks = [slice(r * S_C, (r + 1) * S_C) for r in range(T // S_C)]
    for i in range(D_ATTN // LANES):
        sl = slice(i * LANES, (i + 1) * LANES)
        qr = _rope128(qf[:, sl], c, sa, sb) * Q_SCALE
        k_ref[0, i] = _rope128(kf[:, sl], c, sa, sb).astype(bf16)
        for r, rs in enumerate(chunks):
            qt = qr[rs].T
            qtz_ref[0, r, 2 * i] = jnp.where(sub < HEAD_DIM, qt, 0.0).astype(bf16)
            qtz_ref[0, r, 2 * i + 1] = jnp.where(sub >= HEAD_DIM, qt, 0.0).astype(bf16)
            vt = vf[rs, sl].T.astype(bf16)
            for h in (2 * i, 2 * i + 1):
                vta_ref[0, r, h, 0:HEAD_DIM, :] = vt[(h % 2) * HEAD_DIM:(h % 2 + 1) * HEAD_DIM, :]
                vta_ref[0, r, h, HEAD_DIM:V_ROWS, :] = jnp.ones((V_ROWS - HEAD_DIM, S_C), bf16)
    za = proj(wa_ref, 3)
    gz_ref[0] = za * jax.nn.sigmoid(za)

    pi = jnp.dot(xn, wi_ref[...], preferred_element_type=f32)
    wv = pi[:, 3 * LANES:4 * LANES]
    sgn = jnp.where(wv > 0, 1.0, jnp.where(wv < 0, -1.0, 0.0))
    for r, rs in enumerate(chunks):
        sgt_ref[0, r] = sgn[rs].T[0:SUBLANES, :]
    wabs = jnp.abs(wv) * ((IDX_DIM ** -0.5) * (IDX_HEADS ** -0.5))
    lane = lax.broadcasted_iota(jnp.int32, (T, LANES), 1)
    for i in range(2):
        sl = slice(i * LANES, (i + 1) * LANES)
        scale = jnp.where(lane < IDX_DIM, wabs[:, 2 * i:2 * i + 1], wabs[:, 2 * i + 1:2 * i + 2])
        qi = _rope128(pi[:, sl], c, sa, sb) * scale
        for r, rs in enumerate(chunks):
            qit_ref[0, r, sl, :] = qi[rs].T.astype(bf16)
    ki_ref[0] = _rope128(pi[:, 2 * LANES:3 * LANES], c, sa, sb)[:, 0:IDX_DIM].astype(bf16)


def _attn_kernel(qtz_ref, qit_ref, sgt_ref, gz_ref, k_ref, vta_ref, ki_ref, o_ref,
                 sc_ref, acc_ref, m_ref, l_ref, a0_ref, a1_ref, *, top_k, n_tiles):
    g = pl.program_id(1)
    TQ, SC = T_Q, S_C
    G = SC // SUBLANES
    nslot = sc_ref.shape[0] // 2
    nc = nslot - 1
    i = jnp.minimum(g, n_tiles - 1)
    ip = g - 1
    live = g < n_tiles
    cur = (g % 2) * nslot
    prev = nslot - cur

    def fold(a, op):
        return op(a.reshape(G, SUBLANES, TQ), axis=0)

    def rows(j):
        return pl.ds(pl.multiple_of(j * SC, SC), SC)

    def trips(n):
        return jnp.where(live, n, 0)

    t_loc = lax.broadcasted_iota(jnp.int32, (1, TQ), 1)
    lim_local = ((t_loc // CHUNK) + 1) * CHUNK
    limit = lim_local + i * TQ
    keff = jnp.minimum(limit, top_k).astype(f32)
    adm_diag = lax.broadcasted_iota(jnp.int32, (SC, TQ), 0) < lim_local

    sgt = sgt_ref[0, 0]
    sg = [sgt[h:h + 1, :] for h in range(IDX_HEADS)]

    def score_chunk(j):
        kc = ki_ref[0, rows(j), :]
        s = None
        for h in range(IDX_HEADS):
            l = jnp.dot(kc, qit_ref[0, 0, h * IDX_DIM:(h + 1) * IDX_DIM, :], preferred_element_type=f32)
            t = jnp.maximum(l, 0.0) * sg[h]
            s = t if s is None else s + t
        return s

    minus_inf = jnp.full((SC, TQ), -jnp.inf, f32)
    sc_ref[cur + nc] = minus_inf
    sc_ref[prev + nc] = minus_inf

    def p1_chunks(js, carry):
        mn, mx = carry
        for j in js:
            s = score_chunk(j)
            sc_ref[cur + j] = s
            mn, mx = jnp.minimum(mn, fold(s, jnp.min)), jnp.maximum(mx, fold(s, jnp.max))
        return mn, mx

    mn0 = jnp.full((SUBLANES, TQ), jnp.inf, f32)
    mx0 = jnp.full((SUBLANES, TQ), -jnp.inf, f32)
    U = 4
    carry = lax.fori_loop(0, trips(i // U), lambda jj, c: p1_chunks([U * jj + q for q in range(U)], c), (mn0, mx0))
    mn, mx = lax.fori_loop(trips(i // U * U), trips(i), lambda j, c: p1_chunks([j], c), carry)
    s = score_chunk(i)
    sc_ref[cur + i] = jnp.where(adm_diag, s, -jnp.inf)
    mn = jnp.minimum(mn, fold(jnp.where(adm_diag, s, jnp.inf), jnp.min))
    mx = jnp.maximum(mx, fold(jnp.where(adm_diag, s, -jnp.inf), jnp.max))
    lo = jnp.min(mn, axis=0, keepdims=True)
    hi = jnp.max(mx, axis=0, keepdims=True)

    acc_ref[...] = jnp.zeros(acc_ref.shape, f32)
    m_ref[...] = jnp.full(m_ref.shape, NEG, f32)
    l_ref[...] = jnp.zeros(l_ref.shape, f32)
    P = 2 * SUBLANES

    def all_sublanes(x8, op):
        for shift in (4, 2, 1):
            x8 = op(x8, pltpu.roll(x8, shift, 0))
        return x8

    def kv_chunk(c):
        return jnp.clip(c, 0, jnp.maximum(ip, 0))

    def logits_to(buf, c, between=lambda h: None):
        bias = sc_ref[jnp.where(c <= ip, prev + c, prev + nc)]
        for h in range(N_HEADS):
            kc = k_ref[0, h // 2, rows(kv_chunk(c)), :]
            buf[h] = (jnp.dot(kc, qtz_ref[0, 0, h], preferred_element_type=f32) + bias).astype(bf16)
            between(h)

    def softmax_pv(buf, c, between=lambda h: None):
        for h in range(N_HEADS):
            a = buf[h]
            cmax = jnp.max(a.reshape(SC // P, P, TQ), axis=0).astype(f32)
            m_old = m_ref[h]
            m_new = jnp.maximum(m_old, all_sublanes(jnp.maximum(cmax[0:SUBLANES], cmax[SUBLANES:P]), jnp.maximum))
            alpha = jnp.exp2(m_old - m_new)
            m_ref[h] = m_new
            mb = jnp.concatenate([m_new, m_new], axis=0).astype(bf16)
            p = jnp.exp2(a.reshape(SC // P, P, TQ) - mb[None]).reshape(SC, TQ)
            pv = jnp.dot(vta_ref[0, kv_chunk(c), h], p, preferred_element_type=f32)
            l_ref[h] = alpha * l_ref[h] + pv[HEAD_DIM:HEAD_DIM + SUBLANES]
            acc = acc_ref[h].reshape(HEAD_DIM // SUBLANES, SUBLANES, TQ) * alpha[None]
            acc_ref[h] = acc.reshape(HEAD_DIM, TQ) + pv[0:HEAD_DIM]
            between(h)

    def chunk_count(slot, mid8):
        return jnp.sum(jnp.where(sc_ref[slot].reshape(G, SUBLANES, TQ) >= mid8[None], 1.0, 0.0), axis=0)

    def update(state, mid, c, take=True):
        lo, hi, clo = state
        ge = c >= keff
        up = jnp.logical_and(ge, take)
        down = jnp.logical_and(jnp.logical_not(ge), take)
        return jnp.where(up, mid, lo), jnp.where(down, mid, hi), jnp.where(up, c, clo)

    n = i + 1
    n_trips = (g + 1) // 2
    per_step = (n + RIDE - 1) // RIDE
    n_ride = jnp.where(live, jnp.minimum(n_trips // per_step, N_BISECT), 0)
    logits_to(a0_ref, 0)

    def p4_body(jj, carry):
        lo, hi, clo, cnt, cursor, done = carry
        j = 2 * jj
        active = done < n_ride
        mid = 0.5 * lo + 0.5 * hi
        mid8 = jnp.broadcast_to(mid, (SUBLANES, TQ))

        counts = [cnt]

        def ride(stage):
            per_stage = RIDE // 4

            def between(h):
                for t in range(h * per_stage // N_HEADS, (h + 1) * per_stage // N_HEADS):
                    c = cursor + stage * per_stage + t
                    slot = jnp.where(jnp.logical_and(c <= i, active), cur + c, cur + nc)
                    counts[0] = counts[0] + chunk_count(slot, mid8)
            return between

        logits_to(a1_ref, j + 1, ride(0))
        softmax_pv(a0_ref, j, ride(1))
        logits_to(a0_ref, j + 2, ride(2))
        softmax_pv(a1_ref, j + 1, ride(3))
        cnt = counts[0]
        cursor = cursor + RIDE
        full = jnp.logical_and(cursor >= n, active)
        lo, hi, clo = update((lo, hi, clo), mid, jnp.sum(cnt, axis=0, keepdims=True), take=full)
        keep = jnp.where(full, 0.0, 1.0)
        return lo, hi, clo, cnt * keep, jnp.where(full, 0, cursor), done + full.astype(jnp.int32)

    clo = limit.astype(f32)
    lo, hi, clo, _, _, done = lax.fori_loop(
        0, n_trips, p4_body, (lo, hi, clo, jnp.zeros((SUBLANES, TQ), f32), jnp.int32(0), jnp.int32(0)))

    @pl.when(g >= 1)
    def _():
        for hp in range(N_HEADS // 2):
            pair = jnp.concatenate(
                [(acc_ref[h].reshape(HEAD_DIM // SUBLANES, SUBLANES, TQ) / l_ref[h][None]).reshape(HEAD_DIM, TQ)
                 for h in (2 * hp, 2 * hp + 1)], axis=0)
            sl = slice(hp * LANES, (hp + 1) * LANES)
            o_ref[0, :, sl] = (pair.T * gz_ref[0, :, sl]).astype(bf16)

    def count_ge(mid):
        mid8 = jnp.broadcast_to(mid, (SUBLANES, TQ))

        def body(jj, cnt):
            j2 = 2 * jj + 1
            return (cnt + chunk_count(cur + 2 * jj, mid8)
                    + chunk_count(jnp.where(j2 <= i, cur + j2, cur + nc), mid8))
        cnt = lax.fori_loop(0, i // 2 + 1, body, jnp.zeros((SUBLANES, TQ), f32))
        return jnp.sum(cnt, axis=0, keepdims=True)

    def bisect(state):
        mid = 0.5 * state[0] + 0.5 * state[1]
        return update(state, mid, count_ge(mid))

    def snap(lo, hi):
        def body(jj, carry):
            a, b = carry
            for j in (2 * jj, jnp.minimum(2 * jj + 1, i)):
                s = sc_ref[cur + j]
                a = jnp.minimum(a, fold(jnp.where(s >= lo, s, jnp.inf), jnp.min))
                b = jnp.maximum(b, fold(jnp.where(s <= hi, s, -jnp.inf), jnp.max))
            return a, b
        a, b = lax.fori_loop(0, trips(i // 2 + 1), body, (mn0, mx0))
        return jnp.min(a, axis=0, keepdims=True), jnp.max(b, axis=0, keepdims=True)

    lo, hi, clo = lax.fori_loop(0, trips(N_BISECT - done), lambda _, st: bisect(st), (lo, hi, clo))
    vlo, vhi = snap(lo, hi)

    def unconverged(lo, hi, clo, vlo, vhi):
        bad = jnp.logical_and(clo != keff, vlo != vhi)
        return jnp.logical_and(live, jnp.max(jnp.where(bad, 1.0, 0.0)) > 0.0)

    def refine(state):
        lo, hi, clo, _, _ = state
        lo, hi, clo = bisect((lo, hi, clo))
        vlo, vhi = snap(lo, hi)
        return lo, hi, clo, vlo, vhi

    lo, hi, clo, vlo, vhi = lax.while_loop(lambda st: unconverged(*st), refine, (lo, hi, clo, vlo, vhi))
    tau = vlo
    excess = clo - keff

    ka = lax.broadcasted_iota(jnp.int32, (SC, SC), 0)
    kb = lax.broadcasted_iota(jnp.int32, (SC, SC), 1)
    upper = jnp.where(kb > ka, 1.0, 0.0).astype(bf16)

    def p3_chunks(slots, carry):
        loaded = [(slot, sc_ref[slot]) for slot in slots]
        for slot, s in loaded:
            eq = s == tau
            eqf = jnp.where(eq, 1.0, 0.0)
            later = jnp.dot(upper, eqf.astype(bf16), preferred_element_type=f32) + carry
            sel = jnp.logical_or(s > tau, jnp.logical_and(eq, later >= excess))
            sc_ref[slot] = jnp.where(sel, 0.0, NEG)
            carry = carry + jnp.sum(fold(eqf, jnp.sum), axis=0, keepdims=True)
        return carry

    rest = n % U
    carry = lax.fori_loop(0, trips(rest), lambda r, c: p3_chunks([cur + i - r], c), jnp.zeros((1, TQ), f32))
    lax.fori_loop(0, trips(n // U),
                  lambda jj, c: p3_chunks([cur + i - rest - U * jj - q for q in range(U)], c), carry)


def _out_kernel(x_ref, yc_ref, ya_ref, woc_ref, woa_ref, g_ref, o_ref):
    y = (jnp.dot(yc_ref[0], woc_ref[...], preferred_element_type=f32)
         + jnp.dot(ya_ref[0], woa_ref[...], preferred_element_type=f32))
    yn = y * lax.rsqrt(jnp.mean(y * y, axis=-1, keepdims=True) + EPS) * g_ref[...]
    o_ref[0] = x_ref[0] + yn


def _rope_tables(S):
    half = ROT_DIM // 2
    inv_freq = jnp.power(jnp.float32(ROPE_THETA), -jnp.arange(half, dtype=f32) / half)
    ang = jnp.arange(S, dtype=jnp.int32).astype(f32)[:, None] * inv_freq[None, :]
    cos, sin = jnp.cos(ang), jnp.sin(ang)
    ones = jnp.ones((S, HEAD_DIM - ROT_DIM), f32)
    zeros = jnp.zeros((S, HEAD_DIM - ROT_DIM), f32)
    zh = jnp.zeros((S, half), f32)
    c = jnp.concatenate([cos, cos, ones], axis=1)
    sa = jnp.concatenate([-sin, zh, zeros], axis=1)
    sb = jnp.concatenate([zh, sin, zeros], axis=1)
    rep = LANES // HEAD_DIM
    return jnp.tile(c, (1, rep)), jnp.tile(sa, (1, rep)), jnp.tile(sb, (1, rep))


def _const_spec(shape):
    nd = len(shape)
    return pl.BlockSpec(shape, lambda b, j: (0,) * nd, pipeline_mode=pl.Buffered(1))


def _layer(x, g_pre, w_in, conv_w, w_out, g_post):
    B, S, _ = x.shape
    assert S % T_PROJ == 0 and S % T_Q == 0 and S % T_OUT == 0 and T_Q == S_C and T_PROJ % S_C == 0
    top_k = min(INDEX_TOPK, S // 4)
    nc = S // S_C

    o = 4 * D_CONV + 4 * D_ATTN
    wm = pl.pallas_call(
        _cast_kernel,
        grid=(o // 512,),
        in_specs=[pl.BlockSpec((D_MODEL, 512), lambda c: (0, c))],
        out_specs=pl.BlockSpec((D_MODEL, 512), lambda c: (0, c)),
        out_shape=jax.ShapeDtypeStruct((D_MODEL, o), bf16),
        name="wcast",
    )(w_in)
    nq = IDX_HEADS * IDX_DIM
    wi = jnp.zeros((D_MODEL, IDX_W), f32)
    wi = wi.at[:, 0:nq + IDX_DIM].set(w_in[:, o:o + nq + IDX_DIM])
    wi = wi.at[:, 3 * LANES:3 * LANES + IDX_HEADS].set(w_in[:, o + nq + IDX_DIM:o + nq + IDX_DIM + IDX_HEADS])
    wi = wi.astype(bf16)
    woc = w_out[0:D_CONV].astype(bf16)
    woa = w_out[D_CONV:].astype(bf16)
    c_t, sa_t, sb_t = _rope_tables(S)

    cp = functools.partial(pltpu.CompilerParams, vmem_limit_bytes=VMEM_LIMIT)

    tile = lambda w: pl.BlockSpec((1, T_PROJ, w), lambda b, j: (b, j, 0))
    ttile = lambda *shape: pl.BlockSpec((1, T_PROJ // S_C) + shape, lambda b, j: (b, j) + (0,) * len(shape))
    tab = pl.BlockSpec((T_PROJ, LANES), lambda b, j: (j, 0))
    sds = jax.ShapeDtypeStruct
    yc, qtz, k, vta, gz, qit, ki, sgt = pl.pallas_call(
        _proj_kernel,
        grid=(B, S // T_PROJ),
        in_specs=[tile(D_MODEL), _const_spec((1, D_MODEL)), _const_spec((D_MODEL, 4 * D_CONV + 4 * D_ATTN)),
                  _const_spec((D_MODEL, IDX_W)),
                  _const_spec((CONV_WIDTH, D_CONV)), tab, tab, tab],
        out_specs=[tile(D_CONV), ttile(N_HEADS, LANES, S_C),
                   pl.BlockSpec((1, D_ATTN // LANES, T_PROJ, LANES), lambda b, j: (b, 0, j, 0)),
                   ttile(N_HEADS, V_ROWS, S_C),
                   tile(D_ATTN), ttile(nq, S_C), tile(IDX_DIM), ttile(SUBLANES, S_C)],
        out_shape=[sds((B, S, D_CONV), bf16), sds((B, nc, N_HEADS, LANES, S_C), bf16),
                   sds((B, D_ATTN // LANES, S, LANES), bf16), sds((B, nc, N_HEADS, V_ROWS, S_C), bf16),
                   sds((B, S, D_ATTN), f32), sds((B, nc, nq, S_C), bf16),
                   sds((B, S, IDX_DIM), bf16), sds((B, nc, SUBLANES, S_C), f32)],
        scratch_shapes=[pltpu.VMEM((T_PROJ + 8, D_CONV), f32)],
        compiler_params=cp(dimension_semantics=("arbitrary", "arbitrary")),
        name="proj",
    )(x, g_pre.reshape(1, D_MODEL), wm, wi, conv_w, c_t, sa_t, sb_t)

    nq_tiles = S // T_Q
    lag = lambda g: jnp.maximum(g - 1, 0)
    sel = lambda g: jnp.minimum(g, nq_tiles - 1)
    whole = lambda *shape: pl.BlockSpec((1,) + shape, lambda b, g: (b,) + (0,) * len(shape),
                                        pipeline_mode=pl.Buffered(1))
    ya = pl.pallas_call(
        functools.partial(_attn_kernel, top_k=top_k, n_tiles=nq_tiles),
        grid=(B, nq_tiles + 1),
        in_specs=[pl.BlockSpec((1, 1, N_HEADS, LANES, T_Q), lambda b, g: (b, lag(g), 0, 0, 0)),
                  pl.BlockSpec((1, 1, nq, T_Q), lambda b, g: (b, sel(g), 0, 0)),
                  pl.BlockSpec((1, 1, SUBLANES, T_Q), lambda b, g: (b, sel(g), 0, 0)),
                  pl.BlockSpec((1, T_Q, D_ATTN), lambda b, g: (b, lag(g), 0)),
                  whole(D_ATTN // LANES, S, LANES), whole(nc, N_HEADS, V_ROWS, S_C), whole(S, IDX_DIM)],
        out_specs=pl.BlockSpec((1, T_Q, D_ATTN), lambda b, g: (b, lag(g), 0)),
        out_shape=sds((B, S, D_ATTN), bf16),
        scratch_shapes=[pltpu.VMEM((2 * (nc + 1), S_C, T_Q), f32), pltpu.VMEM((N_HEADS, HEAD_DIM, T_Q), f32),
                        pltpu.VMEM((N_HEADS, SUBLANES, T_Q), f32), pltpu.VMEM((N_HEADS, SUBLANES, T_Q), f32),
                        pltpu.VMEM((N_HEADS, S_C, T_Q), bf16), pltpu.VMEM((N_HEADS, S_C, T_Q), bf16)],
        compiler_params=cp(dimension_semantics=("arbitrary", "arbitrary")),
        name="attn",
    )(qtz, qit, sgt, gz, k, vta, ki)

    otile = lambda w: pl.BlockSpec((1, T_OUT, w), lambda b, j: (b, j, 0))
    return pl.pallas_call(
        _out_kernel,
        grid=(B, S // T_OUT),
        in_specs=[otile(D_MODEL), otile(D_CONV), otile(D_ATTN), _const_spec((D_CONV, D_MODEL)),
                  _const_spec((D_ATTN, D_MODEL)), _const_spec((1, D_MODEL))],
        out_specs=otile(D_MODEL),
        out_shape=sds((B, S, D_MODEL), f32),
        compiler_params=cp(dimension_semantics=("arbitrary", "arbitrary")),
        name="outproj",
    )(x, yc, ya, woc, woa, g_post.reshape(1, D_MODEL))


def kernel(x, norm_pre_g, w_in, conv_w, w_out, norm_post_g):
    for l in range(norm_pre_g.shape[0]):
        x = _layer(x, norm_pre_g[l], w_in[l], conv_w[l], w_out[l], norm_post_g[l])
    return x
```

```python
import functools

import jax
import jax.numpy as jnp
from jax import lax
from jax.experimental import pallas as pl
from jax.experimental.pallas import tpu as pltpu

D_MODEL = 1024
CHUNK = 64
D_CONV = 512
CONV_WIDTH = 3
N_HEADS = 8
HEAD_DIM = 64
D_ATTN = N_HEADS * HEAD_DIM
ROT_DIM = HEAD_DIM // 4
ROPE_THETA = 500000.0
IDX_HEADS = 4
IDX_DIM = 64
INDEX_TOPK = 256
EPS = 1e-6

LANES = 128
SUBLANES = 8
T_PROJ = 512
T_Q = 256
S_C = 256
T_OUT = 1024
IDX_W = 512
N_BISECT = 24
NEG = -1e30
RIDE = 16
V_ROWS = HEAD_DIM + 16
Q_SCALE = HEAD_DIM ** -0.5 * 1.4426950408889634
VMEM_LIMIT = 56 * 1024 * 1024

f32 = jnp.float32
bf16 = jnp.bfloat16


def _rope128(xb, c, sa, sb):
    return xb * c + pltpu.roll(xb, LANES - ROT_DIM // 2, 1) * sa + pltpu.roll(xb, ROT_DIM // 2, 1) * sb


def _proj_kernel(x_ref, g_ref, wm_ref, wi_ref, cw_ref, c_ref, sa_ref, sb_ref,
                 yc_ref, qtz_ref, k_ref, vta_ref, gz_ref, qit_ref, ki_ref, sgt_ref, ubuf):
    j = pl.program_id(1)
    T = T_PROJ

    @pl.when(j == 0)
    def _():
        ubuf[0:8, :] = jnp.zeros((8, D_CONV), f32)

    x = x_ref[0]
    xn = x * lax.rsqrt(jnp.mean(x * x, axis=-1, keepdims=True) + EPS) * g_ref[...]
    xn = xn.astype(bf16)

    wc_ref, wa_ref = 0, 4

    def proj(first, k):
        return jnp.dot(xn, wm_ref[:, (first + k) * 512:(first + k + 1) * 512], preferred_element_type=f32)

    u = proj(wc_ref, 2) * proj(wc_ref, 0)
    ubuf[8:8 + T, :] = u
    conv = (ubuf[6:6 + T, :] * cw_ref[0:1, :] + ubuf[7:7 + T, :] * cw_ref[1:2, :] + u * cw_ref[2:3, :])
    ubuf[0:8, :] = ubuf[T:T + 8, :]
    zc = proj(wc_ref, 3)
    yc_ref[0] = (proj(wc_ref, 1) * conv * (zc * jax.nn.sigmoid(zc))).astype(bf16)

    c = c_ref[...]
    sa = sa_ref[...]
    sb = sb_ref[...]

    qf = proj(wa_ref, 0)
    kf = proj(wa_ref, 1)
    vf = proj(wa_ref, 2)
    sub = lax.broadcasted_iota(jnp.int32, (LANES, S_C), 0)
    chunks = [slice(r * S_C, (r + 1) * S_C) for r in range(T // S_C)]
    for i in range(D_ATTN // LANES):
        sl = slice(i * LANES, (i + 1) * LANES)
        qr = _rope128(qf[:, sl], c, sa, sb) * Q_SCALE
        k_ref[0, i] = _rope128(kf[:, sl], c, sa, sb).astype(bf16)
        for r, rs in enumerate(chunks):
            qt = qr[rs].T
            qtz_ref[0, r, 2 * i] = jnp.where(sub < HEAD_DIM, qt, 0.0).astype(bf16)
            qtz_ref[0, r, 2 * i + 1] = jnp.where(sub >= HEAD_DIM, qt, 0.0).astype(bf16)
            vt = vf[rs, sl].T.astype(bf16)
            for h in (2 * i, 2 * i + 1):
                vta_ref[0, r, h, 0:HEAD_DIM, :] = vt[(h % 2) * HEAD_DIM:(h % 2 + 1) * HEAD_DIM, :]
                vta_ref[0, r, h, HEAD_DIM:V_ROWS, :] = jnp.ones((V_ROWS - HEAD_DIM, S_C), bf16)
    za = proj(wa_ref, 3)
    gz_ref[0] = za * jax.nn.sigmoid(za)

    pi = jnp.dot(xn, wi_ref[...], preferred_element_type=f32)
    wv = pi[:, 3 * LANES:4 * LANES]
    sgn = jnp.where(wv > 0, 1.0, jnp.where(wv < 0, -1.0, 0.0))
    for r, rs in enumerate(chunks):
        sgt_ref[0, r] = sgn[rs].T[0:SUBLANES, :]
    wabs = jnp.abs(wv) * ((IDX_DIM ** -0.5) * (IDX_HEADS ** -0.5))
    lane = lax.broadcasted_iota(jnp.int32, (T, LANES), 1)
    for i in range(2):
        sl = slice(i * LANES, (i + 1) * LANES)
        scale = jnp.where(lane < IDX_DIM, wabs[:, 2 * i:2 * i + 1], wabs[:, 2 * i + 1:2 * i + 2])
        qi = _rope128(pi[:, sl], c, sa, sb) * scale
        for r, rs in enumerate(chunks):
            qit_ref[0, r, sl, :] = qi[rs].T.astype(bf16)
    ki_ref[0] = _rope128(pi[:, 2 * LANES:3 * LANES], c, sa, sb)[:, 0:IDX_DIM].astype(bf16)


def _attn_kernel(qtz_ref, qit_ref, sgt_ref, gz_ref, k_ref, vta_ref, ki_ref, o_ref,
                 sc_ref, acc_ref, m_ref, l_ref, a0_ref, a1_ref, *, top_k, n_tiles):
    g = pl.program_id(1)
    TQ, SC = T_Q, S_C
    G = SC // SUBLANES
    nslot = sc_ref.shape[0] // 2
    nc = nslot - 1
    i = jnp.minimum(g, n_tiles - 1)
    ip = g - 1
    live = g < n_tiles
    cur = (g % 2) * nslot
    prev = nslot - cur

    def fold(a, op):
        return op(a.reshape(G, SUBLANES, TQ), axis=0)

    def rows(j):
        return pl.ds(pl.multiple_of(j * SC, SC), SC)

    def trips(n):
        return jnp.where(live, n, 0)

    t_loc = lax.broadcasted_iota(jnp.int32, (1, TQ), 1)
    lim_local = ((t_loc // CHUNK) + 1) * CHUNK
    limit = lim_local + i * TQ
    keff = jnp.minimum(limit, top_k).astype(f32)
    adm_diag = lax.broadcasted_iota(jnp.int32, (SC, TQ), 0) < lim_local

    sgt = sgt_ref[0, 0]
    sg = [sgt[h:h + 1, :] for h in range(IDX_HEADS)]

    def score_chunk(j):
        kc = ki_ref[0, rows(j), :]
        s = None
        for h in range(IDX_HEADS):
            l = jnp.dot(kc, qit_ref[0, 0, h * IDX_DIM:(h + 1) * IDX_DIM, :], preferred_element_type=f32)
            t = jnp.maximum(l, 0.0) * sg[h]
            s = t if s is None else s + t
        return s

    minus_inf = jnp.full((SC, TQ), -jnp.inf, f32)
    sc_ref[cur + nc] = minus_inf
    sc_ref[prev + nc] = minus_inf

    def p1_chunks(js, carry):
        mn, mx = carry
        for j in js:
            s = score_chunk(j)
            sc_ref[cur + j] = s
            mn, mx = jnp.minimum(mn, fold(s, jnp.min)), jnp.maximum(mx, fold(s, jnp.max))
        return mn, mx

    mn0 = jnp.full((SUBLANES, TQ), jnp.inf, f32)
    mx0 = jnp.full((SUBLANES, TQ), -jnp.inf, f32)
    U = 4
    carry = lax.fori_loop(0, trips(i // U), lambda jj, c: p1_chunks([U * jj + q for q in range(U)], c), (mn0, mx0))
    mn, mx = lax.fori_loop(trips(i // U * U), trips(i), lambda j, c: p1_chunks([j], c), carry)
    s = score_chunk(i)
    sc_ref[cur + i] = jnp.where(adm_diag, s, -jnp.inf)
    mn = jnp.minimum(mn, fold(jnp.where(adm_diag, s, jnp.inf), jnp.min))
    mx = jnp.maximum(mx, fold(jnp.where(adm_diag, s, -jnp.inf), jnp.max))
    lo = jnp.min(mn, axis=0, keepdims=True)
    hi = jnp.max(mx, axis=0, keepdims=True)

    acc_ref[...] = jnp.zeros(acc_ref.shape, f32)
    m_ref[...] = jnp.full(m_ref.shape, NEG, f32)
    l_ref[...] = jnp.zeros(l_ref.shape, f32)
    P = 2 * SUBLANES

    def all_sublanes(x8, op):
        for shift in (4, 2, 1):
            x8 = op(x8, pltpu.roll(x8, shift, 0))
        return x8

    def kv_chunk(c):
        return jnp.clip(c, 0, jnp.maximum(ip, 0))

    def logits_to(buf, c, between=lambda h: None):
        bias = sc_ref[jnp.where(c <= ip, prev + c, prev + nc)]
        for h in range(N_HEADS):
            kc = k_ref[0, h // 2, rows(kv_chunk(c)), :]
            buf[h] = (jnp.dot(kc, qtz_ref[0, 0, h], preferred_element_type=f32) + bias).astype(bf16)
            between(h)

    def softmax_pv(buf, c, between=lambda h: None):
        for h in range(N_HEADS):
            a = buf[h]
            cmax = jnp.max(a.reshape(SC // P, P, TQ), axis=0).astype(f32)
            m_old = m_ref[h]
            m_new = jnp.maximum(m_old, all_sublanes(jnp.maximum(cmax[0:SUBLANES], cmax[SUBLANES:P]), jnp.maximum))
            alpha = jnp.exp2(m_old - m_new)
            m_ref[h] = m_new
            mb = jnp.concatenate([m_new, m_new], axis=0).astype(bf16)
            p = jnp.exp2(a.reshape(SC // P, P, TQ) - mb[None]).reshape(SC, TQ)
            pv = jnp.dot(vta_ref[0, kv_chunk(c), h], p, preferred_element_type=f32)
            l_ref[h] = alpha * l_ref[h] + pv[HEAD_DIM:HEAD_DIM + SUBLANES]
            acc = acc_ref[h].reshape(HEAD_DIM // SUBLANES, SUBLANES, TQ) * alpha[None]
            acc_ref[h] = acc.reshape(HEAD_DIM, TQ) + pv[0:HEAD_DIM]
            between(h)

    def chunk_count(slot, mid8):
        return jnp.sum(jnp.where(sc_ref[slot].reshape(G, SUBLANES, TQ) >= mid8[None], 1.0, 0.0), axis=0)

    def update(state, mid, c, take=True):
        lo, hi, clo = state
        ge = c >= keff
        up = jnp.logical_and(ge, take)
        down = jnp.logical_and(jnp.logical_not(ge), take)
        return jnp.where(up, mid, lo), jnp.where(down, mid, hi), jnp.where(up, c, clo)

    n = i + 1
    n_trips = (g + 1) // 2
    per_step = (n + RIDE - 1) // RIDE
    n_ride = jnp.where(live, jnp.minimum(n_trips // per_step, N_BISECT), 0)
    logits_to(a0_ref, 0)

    def p4_body(jj, carry):
        lo, hi, clo, cnt, cursor, done = carry
        j = 2 * jj
        active = done < n_ride
        mid = 0.5 * lo + 0.5 * hi
        mid8 = jnp.broadcast_to(mid, (SUBLANES, TQ))

        counts = [cnt]

        def ride(stage):
            per_stage = RIDE // 4

            def between(h):
                for t in range(h * per_stage // N_HEADS, (h + 1) * per_stage // N_HEADS):
                    c = cursor + stage * per_stage + t
                    slot = jnp.where(jnp.logical_and(c <= i, active), cur + c, cur + nc)
                    counts[0] = counts[0] + chunk_count(slot, mid8)
            return between

        logits_to(a1_ref, j + 1, ride(0))
        softmax_pv(a0_ref, j, ride(1))
        logits_to(a0_ref, j + 2, ride(2))
        softmax_pv(a1_ref, j + 1, ride(3))
        cnt = counts[0]
        cursor = cursor + RIDE
        full = jnp.logical_and(cursor >= n, active)
        lo, hi, clo = update((lo, hi, clo), mid, jnp.sum(cnt, axis=0, keepdims=True), take=full)
        keep = jnp.where(full, 0.0, 1.0)
        return lo, hi, clo, cnt * keep, jnp.where(full, 0, cursor), done + full.astype(jnp.int32)

    clo = limit.astype(f32)
    lo, hi, clo, _, _, done = lax.fori_loop(
        0, n_trips, p4_body, (lo, hi, clo, jnp.zeros((SUBLANES, TQ), f32), jnp.int32(0), jnp.int32(0)))

    @pl.when(g >= 1)
    def _():
        for hp in range(N_HEADS // 2):
            pair = jnp.concatenate(
                [(acc_ref[h].reshape(HEAD_DIM // SUBLANES, SUBLANES, TQ) / l_ref[h][None]).reshape(HEAD_DIM, TQ)
                 for h in (2 * hp, 2 * hp + 1)], axis=0)
            sl = slice(hp * LANES, (hp + 1) * LANES)
            o_ref[0, :, sl] = (pair.T * gz_ref[0, :, sl]).astype(bf16)

    def count_ge(mid):
        mid8 = jnp.broadcast_to(mid, (SUBLANES, TQ))

        def body(jj, cnt):
            j2 = 2 * jj + 1
            return (cnt + chunk_count(cur + 2 * jj, mid8)
                    + chunk_count(jnp.where(j2 <= i, cur + j2, cur + nc), mid8))
        cnt = lax.fori_loop(0, i // 2 + 1, body, jnp.zeros((SUBLANES, TQ), f32))
        return jnp.sum(cnt, axis=0, keepdims=True)

    def bisect(state):
        mid = 0.5 * state[0] + 0.5 * state[1]
        return update(state, mid, count_ge(mid))

    def snap(lo, hi):
        def body(jj, carry):
            a, b = carry
            for j in (2 * jj, jnp.minimum(2 * jj + 1, i)):
                s = sc_ref[cur + j]
                a = jnp.minimum(a, fold(jnp.where(s >= lo, s, jnp.inf), jnp.min))
                b = jnp.maximum(b, fold(jnp.where(s <= hi, s, -jnp.inf), jnp.max))
            return a, b
        a, b = lax.fori_loop(0, trips(i // 2 + 1), body, (mn0, mx0))
        return jnp.min(a, axis=0, keepdims=True), jnp.max(b, axis=0, keepdims=True)

    lo, hi, clo = lax.fori_loop(0, trips(N_BISECT - done), lambda _, st: bisect(st), (lo, hi, clo))
    vlo, vhi = snap(lo, hi)

    def unconverged(lo, hi, clo, vlo, vhi):
        bad = jnp.logical_and(clo != keff, vlo != vhi)
        return jnp.logical_and(live, jnp.max(jnp.where(bad, 1.0, 0.0)) > 0.0)

    def refine(state):
        lo, hi, clo, _, _ = state
        lo, hi, clo = bisect((lo, hi, clo))
        vlo, vhi = snap(lo, hi)
        return lo, hi, clo, vlo, vhi

    lo, hi, clo, vlo, vhi = lax.while_loop(lambda st: unconverged(*st), refine, (lo, hi, clo, vlo, vhi))
    tau = vlo
    excess = clo - keff

    ka = lax.broadcasted_iota(jnp.int32, (SC, SC), 0)
    kb = lax.broadcasted_iota(jnp.int32, (SC, SC), 1)
    upper = jnp.where(kb > ka, 1.0, 0.0).astype(bf16)

    def p3_chunks(slots, carry):
        loaded = [(slot, sc_ref[slot]) for slot in slots]
        for slot, s in loaded:
            eq = s == tau
            eqf = jnp.where(eq, 1.0, 0.0)
            later = jnp.dot(upper, eqf.astype(bf16), preferred_element_type=f32) + carry
            sel = jnp.logical_or(s > tau, jnp.logical_and(eq, later >= excess))
            sc_ref[slot] = jnp.where(sel, 0.0, NEG)
            carry = carry + jnp.sum(fold(eqf, jnp.sum), axis=0, keepdims=True)
        return carry

    rest = n % U
    carry = lax.fori_loop(0, trips(rest), lambda r, c: p3_chunks([cur + i - r], c), jnp.zeros((1, TQ), f32))
    lax.fori_loop(0, trips(n // U),
                  lambda jj, c: p3_chunks([cur + i - rest - U * jj - q for q in range(U)], c), carry)


def _out_kernel(x_ref, yc_ref, ya_ref, woc_ref, woa_ref, g_ref, o_ref):
    y = (jnp.dot(yc_ref[0], woc_ref[...], preferred_element_type=f32)
         + jnp.dot(ya_ref[0], woa_ref[...], preferred_element_type=f32))
    yn = y * lax.rsqrt(jnp.mean(y * y, axis=-1, keepdims=True) + EPS) * g_ref[...]
    o_ref[0] = x_ref[0] + yn


def _rope_tables(S):
    half = ROT_DIM // 2
    inv_freq = jnp.power(jnp.float32(ROPE_THETA), -jnp.arange(half, dtype=f32) / half)
    ang = jnp.arange(S, dtype=jnp.int32).astype(f32)[:, None] * inv_freq[None, :]
    cos, sin = jnp.cos(ang), jnp.sin(ang)
    ones = jnp.ones((S, HEAD_DIM - ROT_DIM), f32)
    zeros = jnp.zeros((S, HEAD_DIM - ROT_DIM), f32)
    zh = jnp.zeros((S, half), f32)
    c = jnp.concatenate([cos, cos, ones], axis=1)
    sa = jnp.concatenate([-sin, zh, zeros], axis=1)
    sb = jnp.concatenate([zh, sin, zeros], axis=1)
    rep = LANES // HEAD_DIM
    return jnp.tile(c, (1, rep)), jnp.tile(sa, (1, rep)), jnp.tile(sb, (1, rep))


def _const_spec(shape):
    nd = len(shape)
    return pl.BlockSpec(shape, lambda b, j: (0,) * nd, pipeline_mode=pl.Buffered(1))


def _layer(x, g_pre, w_in, conv_w, w_out, g_post):
    B, S, _ = x.shape
    assert S % T_PROJ == 0 and S % T_Q == 0 and S % T_OUT == 0 and T_Q == S_C and T_PROJ % S_C == 0
    top_k = min(INDEX_TOPK, S // 4)
    nc = S // S_C

    o = 4 * D_CONV + 4 * D_ATTN
    wm = w_in[:, 0:o]
    nq = IDX_HEADS * IDX_DIM
    wi = jnp.zeros((D_MODEL, IDX_W), bf16)
    wi = wi.at[:, 0:nq + IDX_DIM].set(w_in[:, o:o + nq + IDX_DIM])
    wi = wi.at[:, 3 * LANES:3 * LANES + IDX_HEADS].set(w_in[:, o + nq + IDX_DIM:o + nq + IDX_DIM + IDX_HEADS])
    woc = w_out[0:D_CONV]
    woa = w_out[D_CONV:]
    c_t, sa_t, sb_t = _rope_tables(S)

    cp = functools.partial(pltpu.CompilerParams, vmem_limit_bytes=VMEM_LIMIT)

    tile = lambda w: pl.BlockSpec((1, T_PROJ, w), lambda b, j: (b, j, 0))
    ttile = lambda *shape: pl.BlockSpec((1, T_PROJ // S_C) + shape, lambda b, j: (b, j) + (0,) * len(shape))
    tab = pl.BlockSpec((T_PROJ, LANES), lambda b, j: (j, 0))
    sds = jax.ShapeDtypeStruct
    yc, qtz, k, vta, gz, qit, ki, sgt = pl.pallas_call(
        _proj_kernel,
        grid=(B, S // T_PROJ),
        in_specs=[tile(D_MODEL), _const_spec((1, D_MODEL)), _const_spec((D_MODEL, 4 * D_CONV + 4 * D_ATTN)),
                  _const_spec((D_MODEL, IDX_W)),
                  _const_spec((CONV_WIDTH, D_CONV)), tab, tab, tab],
        out_specs=[tile(D_CONV), ttile(N_HEADS, LANES, S_C),
                   pl.BlockSpec((1, D_ATTN // LANES, T_PROJ, LANES), lambda b, j: (b, 0, j, 0)),
                   ttile(N_HEADS, V_ROWS, S_C),
                   tile(D_ATTN), ttile(nq, S_C), tile(IDX_DIM), ttile(SUBLANES, S_C)],
        out_shape=[sds((B, S, D_CONV), bf16), sds((B, nc, N_HEADS, LANES, S_C), bf16),
                   sds((B, D_ATTN // LANES, S, LANES), bf16), sds((B, nc, N_HEADS, V_ROWS, S_C), bf16),
                   sds((B, S, D_ATTN), f32), sds((B, nc, nq, S_C), bf16),
                   sds((B, S, IDX_DIM), bf16), sds((B, nc, SUBLANES, S_C), f32)],
        scratch_shapes=[pltpu.VMEM((T_PROJ + 8, D_CONV), f32)],
        compiler_params=cp(dimension_semantics=("arbitrary", "arbitrary")),
        name="proj",
    )(x, g_pre.reshape(1, D_MODEL), wm, wi, conv_w, c_t, sa_t, sb_t)

    nq_tiles = S // T_Q
    lag = lambda g: jnp.maximum(g - 1, 0)
    sel = lambda g: jnp.minimum(g, nq_tiles - 1)
    whole = lambda *shape: pl.BlockSpec((1,) + shape, lambda b, g: (b,) + (0,) * len(shape),
                                        pipeline_mode=pl.Buffered(1))
    ya = pl.pallas_call(
        functools.partial(_attn_kernel, top_k=top_k, n_tiles=nq_tiles),
        grid=(B, nq_tiles + 1),
        in_specs=[pl.BlockSpec((1, 1, N_HEADS, LANES, T_Q), lambda b, g: (b, lag(g), 0, 0, 0)),
                  pl.BlockSpec((1, 1, nq, T_Q), lambda b, g: (b, sel(g), 0, 0)),
                  pl.BlockSpec((1, 1, SUBLANES, T_Q), lambda b, g: (b, sel(g), 0, 0)),
                  pl.BlockSpec((1, T_Q, D_ATTN), lambda b, g: (b, lag(g), 0)),
                  whole(D_ATTN // LANES, S, LANES), whole(nc, N_HEADS, V_ROWS, S_C), whole(S, IDX_DIM)],
        out_specs=pl.BlockSpec((1, T_Q, D_ATTN), lambda b, g: (b, lag(g), 0)),
        out_shape=sds((B, S, D_ATTN), bf16),
        scratch_shapes=[pltpu.VMEM((2 * (nc + 1), S_C, T_Q), f32), pltpu.VMEM((N_HEADS, HEAD_DIM, T_Q), f32),
                        pltpu.VMEM((N_HEADS, SUBLANES, T_Q), f32), pltpu.VMEM((N_HEADS, SUBLANES, T_Q), f32),
                        pltpu.VMEM((N_HEADS, S_C, T_Q), bf16), pltpu.VMEM((N_HEADS, S_C, T_Q), bf16)],
        compiler_params=cp(dimension_semantics=("arbitrary", "arbitrary")),
        name="attn",
    )(qtz, qit, sgt, gz, k, vta, ki)

    otile = lambda w: pl.BlockSpec((1, T_OUT, w), lambda b, j: (b, j, 0))
    return pl.pallas_call(
        _out_kernel,
        grid=(B, S // T_OUT),
        in_specs=[otile(D_MODEL), otile(D_CONV), otile(D_ATTN), _const_spec((D_CONV, D_MODEL)),
                  _const_spec((D_ATTN, D_MODEL)), _const_spec((1, D_MODEL))],
        out_specs=otile(D_MODEL),
        out_shape=sds((B, S, D_MODEL), f32),
        compiler_params=cp(dimension_semantics=("arbitrary", "arbitrary")),
        name="outproj",
    )(x, yc, ya, woc, woa, g_post.reshape(1, D_MODEL))


def kernel(x, norm_pre_g, w_in, conv_w, w_out, norm_post_g):
    w_in, w_out = w_in.astype(bf16), w_out.astype(bf16)
    for l in range(norm_pre_g.shape[0]):
        x = _layer(x, norm_pre_g[l], w_in[l], conv_w[l], w_out[l], norm_post_g[l])
    return x
```

```python
import functools

import jax
import jax.numpy as jnp
from jax import lax
from jax.experimental import pallas as pl
from jax.experimental.pallas import tpu as pltpu

D_MODEL = 1024
CHUNK = 64
D_CONV = 512
CONV_WIDTH = 3
N_HEADS = 8
HEAD_DIM = 64
D_ATTN = N_HEADS * HEAD_DIM
ROT_DIM = HEAD_DIM // 4
ROPE_THETA = 500000.0
IDX_HEADS = 4
IDX_DIM = 64
INDEX_TOPK = 256
EPS = 1e-6

LANES = 128
SUBLANES = 8
T_PROJ = 512
T_Q = 256
S_C = 256
T_OUT = 1024
IDX_W = 512
N_BISECT = 22
NEG = -1e30
RIDE = 16
V_ROWS = HEAD_DIM + 16
Q_SCALE = HEAD_DIM ** -0.5 * 1.4426950408889634
VMEM_LIMIT = 56 * 1024 * 1024

f32 = jnp.float32
bf16 = jnp.bfloat16


def _rope128(xb, c, sa, sb):
    return xb * c + pltpu.roll(xb, LANES - ROT_DIM // 2, 1) * sa + pltpu.roll(xb, ROT_DIM // 2, 1) * sb


def _proj_kernel(x_ref, g_ref, wm_ref, wi_ref, cw_ref, c_ref, sa_ref, sb_ref,
                 yc_ref, qtz_ref, k_ref, vta_ref, gz_ref, qit_ref, ki_ref, sgt_ref, ubuf):
    j = pl.program_id(1)
    T = T_PROJ

    @pl.when(j == 0)
    def _():
        ubuf[0:8, :] = jnp.zeros((8, D_CONV), f32)

    x = x_ref[0]
    xn = x * lax.rsqrt(jnp.mean(x * x, axis=-1, keepdims=True) + EPS) * g_ref[...]
    xn = xn.astype(bf16)

    wc_ref, wa_ref = 0, 4

    def proj(first, k):
        w = wm_ref[(first + k) * 512:(first + k + 1) * 512, :]
        return lax.dot_general(xn, w, (((1,), (1,)), ((), ())), preferred_element_type=f32)

    u = proj(wc_ref, 2) * proj(wc_ref, 0)
    ubuf[8:8 + T, :] = u
    conv = (ubuf[6:6 + T, :] * cw_ref[0:1, :] + ubuf[7:7 + T, :] * cw_ref[1:2, :] + u * cw_ref[2:3, :])
    ubuf[0:8, :] = ubuf[T:T + 8, :]
    zc = proj(wc_ref, 3)
    yc_ref[0] = (proj(wc_ref, 1) * conv * (zc * jax.nn.sigmoid(zc))).astype(bf16)

    c = c_ref[...]
    sa = sa_ref[...]
    sb = sb_ref[...]

    qf = proj(wa_ref, 0)
    kf = proj(wa_ref, 1)
    vf = proj(wa_ref, 2)
    sub = lax.broadcasted_iota(jnp.int32, (LANES, S_C), 0)
    chunks = [slice(r * S_C, (r + 1) * S_C) for r in range(T // S_C)]
    for i in range(D_ATTN // LANES):
        sl = slice(i * LANES, (i + 1) * LANES)
        qr = _rope128(qf[:, sl], c, sa, sb) * Q_SCALE
        k_ref[0, i] = _rope128(kf[:, sl], c, sa, sb).astype(bf16)
        for r, rs in enumerate(chunks):
            qt = qr[rs].T
            qtz_ref[0, r, 2 * i] = jnp.where(sub < HEAD_DIM, qt, 0.0).astype(bf16)
            qtz_ref[0, r, 2 * i + 1] = jnp.where(sub >= HEAD_DIM, qt, 0.0).astype(bf16)
            vt = vf[rs, sl].T.astype(bf16)
            for h in (2 * i, 2 * i + 1):
                vta_ref[0, r, h, 0:HEAD_DIM, :] = vt[(h % 2) * HEAD_DIM:(h % 2 + 1) * HEAD_DIM, :]
                vta_ref[0, r, h, HEAD_DIM:V_ROWS, :] = jnp.ones((V_ROWS - HEAD_DIM, S_C), bf16)
    za = proj(wa_ref, 3)
    gz_ref[0] = za * jax.nn.sigmoid(za)

    pi = jnp.dot(xn, wi_ref[...], preferred_element_type=f32)
    wv = pi[:, 3 * LANES:4 * LANES]
    sgn = jnp.where(wv > 0, 1.0, jnp.where(wv < 0, -1.0, 0.0))
    for r, rs in enumerate(chunks):
        sgt_ref[0, r] = sgn[rs].T[0:SUBLANES, :]
    wabs = jnp.abs(wv) * ((IDX_DIM ** -0.5) * (IDX_HEADS ** -0.5))
    lane = lax.broadcasted_iota(jnp.int32, (T, LANES), 1)
    for i in range(2):
        sl = slice(i * LANES, (i + 1) * LANES)
        scale = jnp.where(lane < IDX_DIM, wabs[:, 2 * i:2 * i + 1], wabs[:, 2 * i + 1:2 * i + 2])
        qi = _rope128(pi[:, sl], c, sa, sb) * scale
        for r, rs in enumerate(chunks):
            qit_ref[0, r, sl, :] = qi[rs].T.astype(bf16)
    ki_ref[0] = _rope128(pi[:, 2 * LANES:3 * LANES], c, sa, sb)[:, 0:IDX_DIM].astype(bf16)


def _attn_kernel(qtz_ref, qit_ref, sgt_ref, gz_ref, k_ref, vta_ref, ki_ref, o_ref,
                 sc_ref, acc_ref, m_ref, l_ref, a0_ref, a1_ref, *, top_k, n_tiles):
    g = pl.program_id(1)
    TQ, SC = T_Q, S_C
    G = SC // SUBLANES
    nslot = sc_ref.shape[0] // 2
    nc = nslot - 1
    i = jnp.minimum(g, n_tiles - 1)
    ip = g - 1
    live = g < n_tiles
    cur = (g % 2) * nslot
    prev = nslot - cur

    def fold(a, op):
        return op(a.reshape(G, SUBLANES, TQ), axis=0)

    def rows(j):
        return pl.ds(pl.multiple_of(j * SC, SC), SC)

    def trips(n):
        return jnp.where(live, n, 0)

    t_loc = lax.broadcasted_iota(jnp.int32, (1, TQ), 1)
    lim_local = ((t_loc // CHUNK) + 1) * CHUNK
    limit = lim_local + i * TQ
    keff = jnp.minimum(limit, top_k).astype(f32)
    adm_diag = lax.broadcasted_iota(jnp.int32, (SC, TQ), 0) < lim_local

    sgt = sgt_ref[0, 0]
    sg = [sgt[h:h + 1, :] for h in range(IDX_HEADS)]

    def score_chunk(j):
        kc = ki_ref[0, rows(j), :]
        s = None
        for h in range(IDX_HEADS):
            l = jnp.dot(kc, qit_ref[0, 0, h * IDX_DIM:(h + 1) * IDX_DIM, :], preferred_element_type=f32)
            t = jnp.maximum(l, 0.0) * sg[h]
            s = t if s is None else s + t
        return s

    minus_inf = jnp.full((SC, TQ), -jnp.inf, f32)
    sc_ref[cur + nc] = minus_inf
    sc_ref[prev + nc] = minus_inf

    def p1_chunks(js, carry):
        mn, mx = carry
        for j in js:
            s = score_chunk(j)
            sc_ref[cur + j] = s
            mn, mx = jnp.minimum(mn, fold(s, jnp.min)), jnp.maximum(mx, fold(s, jnp.max))
        return mn, mx

    mn0 = jnp.full((SUBLANES, TQ), jnp.inf, f32)
    mx0 = jnp.full((SUBLANES, TQ), -jnp.inf, f32)
    U = 4
    carry = lax.fori_loop(0, trips(i // U), lambda jj, c: p1_chunks([U * jj + q for q in range(U)], c), (mn0, mx0))
    mn, mx = lax.fori_loop(trips(i // U * U), trips(i), lambda j, c: p1_chunks([j], c), carry)
    s = score_chunk(i)
    sc_ref[cur + i] = jnp.where(adm_diag, s, -jnp.inf)
    mn = jnp.minimum(mn, fold(jnp.where(adm_diag, s, jnp.inf), jnp.min))
    mx = jnp.maximum(mx, fold(jnp.where(adm_diag, s, -jnp.inf), jnp.max))
    lo = jnp.min(mn, axis=0, keepdims=True)
    hi = jnp.max(mx, axis=0, keepdims=True)

    acc_ref[...] = jnp.zeros(acc_ref.shape, f32)
    m_ref[...] = jnp.full(m_ref.shape, NEG, f32)
    l_ref[...] = jnp.zeros(l_ref.shape, f32)
    P = 2 * SUBLANES

    def all_sublanes(x8, op):
        for shift in (4, 2, 1):
            x8 = op(x8, pltpu.roll(x8, shift, 0))
        return x8

    def kv_chunk(c):
        return jnp.clip(c, 0, jnp.maximum(ip, 0))

    def logits_to(buf, c, between=lambda h: None):
        bias = sc_ref[jnp.where(c <= ip, prev + c, prev + nc)]
        for h in range(N_HEADS):
            kc = k_ref[0, h // 2, rows(kv_chunk(c)), :]
            buf[h] = (jnp.dot(kc, qtz_ref[0, 0, h], preferred_element_type=f32) + bias).astype(bf16)
            between(h)

    def softmax_pv(buf, c, between=lambda h: None):
        for h in range(N_HEADS):
            a = buf[h]
            cmax = jnp.max(a.reshape(SC // P, P, TQ), axis=0).astype(f32)
            m_old = m_ref[h]
            m_new = jnp.maximum(m_old, all_sublanes(jnp.maximum(cmax[0:SUBLANES], cmax[SUBLANES:P]), jnp.maximum))
            alpha = jnp.exp2(m_old - m_new)
            m_ref[h] = m_new
            mb = jnp.concatenate([m_new, m_new], axis=0).astype(bf16)
            p = jnp.exp2(a.reshape(SC // P, P, TQ) - mb[None]).reshape(SC, TQ)
            pv = jnp.dot(vta_ref[0, kv_chunk(c), h], p, preferred_element_type=f32)
            l_ref[h] = alpha * l_ref[h] + pv[HEAD_DIM:HEAD_DIM + SUBLANES]
            acc = acc_ref[h].reshape(HEAD_DIM // SUBLANES, SUBLANES, TQ) * alpha[None]
            acc_ref[h] = acc.reshape(HEAD_DIM, TQ) + pv[0:HEAD_DIM]
            between(h)

    def chunk_count(slot, mid8):
        return jnp.sum(jnp.where(sc_ref[slot].reshape(G, SUBLANES, TQ) >= mid8[None], 1.0, 0.0), axis=0)

    def update(state, mid, c, take=True):
        lo, hi, clo = state
        ge = c >= keff
        up = jnp.logical_and(ge, take)
        down = jnp.logical_and(jnp.logical_not(ge), take)
        return jnp.where(up, mid, lo), jnp.where(down, mid, hi), jnp.where(up, c, clo)

    n = i + 1
    n_trips = (g + 1) // 2
    per_step = (n + RIDE - 1) // RIDE
    n_ride = jnp.where(live, jnp.minimum(n_trips // per_step, N_BISECT), 0)
    logits_to(a0_ref, 0)

    def p4_body(jj, carry):
        lo, hi, clo, cnt, cursor, done = carry
        j = 2 * jj
        active = done < n_ride
        mid = 0.5 * lo + 0.5 * hi
        mid8 = jnp.broadcast_to(mid, (SUBLANES, TQ))

        counts = [cnt]

        def ride(stage):
            per_stage = RIDE // 4

            def between(h):
                for t in range(h * per_stage // N_HEADS, (h + 1) * per_stage // N_HEADS):
                    c = cursor + stage * per_stage + t
                    slot = jnp.where(jnp.logical_and(c <= i, active), cur + c, cur + nc)
                    counts[0] = counts[0] + chunk_count(slot, mid8)
            return between

        logits_to(a1_ref, j + 1, ride(0))
        softmax_pv(a0_ref, j, ride(1))
        logits_to(a0_ref, j + 2, ride(2))
        softmax_pv(a1_ref, j + 1, ride(3))
        cnt = counts[0]
        cursor = cursor + RIDE
        full = jnp.logical_and(cursor >= n, active)
        lo, hi, clo = update((lo, hi, clo), mid, jnp.sum(cnt, axis=0, keepdims=True), take=full)
        keep = jnp.where(full, 0.0, 1.0)
        return lo, hi, clo, cnt * keep, jnp.where(full, 0, cursor), done + full.astype(jnp.int32)

    clo = limit.astype(f32)
    lo, hi, clo, _, _, done = lax.fori_loop(
        0, n_trips, p4_body, (lo, hi, clo, jnp.zeros((SUBLANES, TQ), f32), jnp.int32(0), jnp.int32(0)))

    @pl.when(g >= 1)
    def _():
        for hp in range(N_HEADS // 2):
            pair = jnp.concatenate(
                [(acc_ref[h].reshape(HEAD_DIM // SUBLANES, SUBLANES, TQ) / l_ref[h][None]).reshape(HEAD_DIM, TQ)
                 for h in (2 * hp, 2 * hp + 1)], axis=0)
            sl = slice(hp * LANES, (hp + 1) * LANES)
            o_ref[0, :, sl] = (pair.T * gz_ref[0, :, sl]).astype(bf16)

    def count_ge(mid):
        mid8 = jnp.broadcast_to(mid, (SUBLANES, TQ))

        def body(jj, cnt):
            j2 = 2 * jj + 1
            return (cnt + chunk_count(cur + 2 * jj, mid8)
                    + chunk_count(jnp.where(j2 <= i, cur + j2, cur + nc), mid8))
        cnt = lax.fori_loop(0, i // 2 + 1, body, jnp.zeros((SUBLANES, TQ), f32))
        return jnp.sum(cnt, axis=0, keepdims=True)

    def bisect(state):
        mid = 0.5 * state[0] + 0.5 * state[1]
        return update(state, mid, count_ge(mid))

    def snap(lo, hi):
        def body(jj, carry):
            a, b = carry
            for j in (2 * jj, jnp.minimum(2 * jj + 1, i)):
                s = sc_ref[cur + j]
                a = jnp.minimum(a, fold(jnp.where(s >= lo, s, jnp.inf), jnp.min))
                b = jnp.maximum(b, fold(jnp.where(s <= hi, s, -jnp.inf), jnp.max))
            return a, b
        a, b = lax.fori_loop(0, trips(i // 2 + 1), body, (mn0, mx0))
        return jnp.min(a, axis=0, keepdims=True), jnp.max(b, axis=0, keepdims=True)

    lo, hi, clo = lax.fori_loop(0, trips(N_BISECT - done), lambda _, st: bisect(st), (lo, hi, clo))
    vlo, vhi = snap(lo, hi)

    def unconverged(lo, hi, clo, vlo, vhi):
        bad = jnp.logical_and(clo != keff, vlo != vhi)
        return jnp.logical_and(live, jnp.max(jnp.where(bad, 1.0, 0.0)) > 0.0)

    def refine(state):
        lo, hi, clo, _, _ = state
        lo, hi, clo = bisect((lo, hi, clo))
        vlo, vhi = snap(lo, hi)
        return lo, hi, clo, vlo, vhi

    lo, hi, clo, vlo, vhi = lax.while_loop(lambda st: unconverged(*st), refine, (lo, hi, clo, vlo, vhi))
    tau = vlo
    excess = clo - keff

    ka = lax.broadcasted_iota(jnp.int32, (SC, SC), 0)
    kb = lax.broadcasted_iota(jnp.int32, (SC, SC), 1)
    upper = jnp.where(kb > ka, 1.0, 0.0).astype(bf16)

    def p3_chunks(slots, carry):
        loaded = [(slot, sc_ref[slot]) for slot in slots]
        for slot, s in loaded:
            eq = s == tau
            eqf = jnp.where(eq, 1.0, 0.0)
            later = jnp.dot(upper, eqf.astype(bf16), preferred_element_type=f32) + carry
            sel = jnp.logical_or(s > tau, jnp.logical_and(eq, later >= excess))
            sc_ref[slot] = jnp.where(sel, 0.0, NEG)
            carry = carry + jnp.sum(fold(eqf, jnp.sum), axis=0, keepdims=True)
        return carry

    rest = n % U
    carry = lax.fori_loop(0, trips(rest), lambda r, c: p3_chunks([cur + i - r], c), jnp.zeros((1, TQ), f32))
    lax.fori_loop(0, trips(n // U),
                  lambda jj, c: p3_chunks([cur + i - rest - U * jj - q for q in range(U)], c), carry)


def _out_kernel(x_ref, yc_ref, ya_ref, woc_ref, woa_ref, g_ref, o_ref):
    y = (jnp.dot(yc_ref[0], woc_ref[...], preferred_element_type=f32)
         + jnp.dot(ya_ref[0], woa_ref[...], preferred_element_type=f32))
    yn = y * lax.rsqrt(jnp.mean(y * y, axis=-1, keepdims=True) + EPS) * g_ref[...]
    o_ref[0] = x_ref[0] + yn


def _rope_tables(S):
    half = ROT_DIM // 2
    inv_freq = jnp.power(jnp.float32(ROPE_THETA), -jnp.arange(half, dtype=f32) / half)
    ang = jnp.arange(S, dtype=jnp.int32).astype(f32)[:, None] * inv_freq[None, :]
    cos, sin = jnp.cos(ang), jnp.sin(ang)
    ones = jnp.ones((S, HEAD_DIM - ROT_DIM), f32)
    zeros = jnp.zeros((S, HEAD_DIM - ROT_DIM), f32)
    zh = jnp.zeros((S, half), f32)
    c = jnp.concatenate([cos, cos, ones], axis=1)
    sa = jnp.concatenate([-sin, zh, zeros], axis=1)
    sb = jnp.concatenate([zh, sin, zeros], axis=1)
    rep = LANES // HEAD_DIM
    return jnp.tile(c, (1, rep)), jnp.tile(sa, (1, rep)), jnp.tile(sb, (1, rep))


def _const_spec(shape):
    nd = len(shape)
    return pl.BlockSpec(shape, lambda b, j: (0,) * nd, pipeline_mode=pl.Buffered(1))


def _layer(x, g_pre, w_in, conv_w, w_out, g_post):
    B, S, _ = x.shape
    assert S % T_PROJ == 0 and S % T_Q == 0 and S % T_OUT == 0 and T_Q == S_C and T_PROJ % S_C == 0
    top_k = min(INDEX_TOPK, S // 4)
    nc = S // S_C

    o = 4 * D_CONV + 4 * D_ATTN
    wm = jnp.transpose(w_in)[0:o]
    nq = IDX_HEADS * IDX_DIM
    wi = jnp.zeros((D_MODEL, IDX_W), bf16)
    wi = wi.at[:, 0:nq + IDX_DIM].set(w_in[:, o:o + nq + IDX_DIM])
    wi = wi.at[:, 3 * LANES:3 * LANES + IDX_HEADS].set(w_in[:, o + nq + IDX_DIM:o + nq + IDX_DIM + IDX_HEADS])
    woc = w_out[0:D_CONV]
    woa = w_out[D_CONV:]
    c_t, sa_t, sb_t = _rope_tables(S)

    cp = functools.partial(pltpu.CompilerParams, vmem_limit_bytes=VMEM_LIMIT)

    tile = lambda w: pl.BlockSpec((1, T_PROJ, w), lambda b, j: (b, j, 0))
    ttile = lambda *shape: pl.BlockSpec((1, T_PROJ // S_C) + shape, lambda b, j: (b, j) + (0,) * len(shape))
    tab = pl.BlockSpec((T_PROJ, LANES), lambda b, j: (j, 0))
    sds = jax.ShapeDtypeStruct
    yc, qtz, k, vta, gz, qit, ki, sgt = pl.pallas_call(
        _proj_kernel,
        grid=(B, S // T_PROJ),
        in_specs=[tile(D_MODEL), _const_spec((1, D_MODEL)), _const_spec((4 * D_CONV + 4 * D_ATTN, D_MODEL)),
                  _const_spec((D_MODEL, IDX_W)),
                  _const_spec((CONV_WIDTH, D_CONV)), tab, tab, tab],
        out_specs=[tile(D_CONV), ttile(N_HEADS, LANES, S_C),
                   pl.BlockSpec((1, D_ATTN // LANES, T_PROJ, LANES), lambda b, j: (b, 0, j, 0)),
                   ttile(N_HEADS, V_ROWS, S_C),
                   tile(D_ATTN), ttile(nq, S_C), tile(IDX_DIM), ttile(SUBLANES, S_C)],
        out_shape=[sds((B, S, D_CONV), bf16), sds((B, nc, N_HEADS, LANES, S_C), bf16),
                   sds((B, D_ATTN // LANES, S, LANES), bf16), sds((B, nc, N_HEADS, V_ROWS, S_C), bf16),
                   sds((B, S, D_ATTN), f32), sds((B, nc, nq, S_C), bf16),
                   sds((B, S, IDX_DIM), bf16), sds((B, nc, SUBLANES, S_C), f32)],
        scratch_shapes=[pltpu.VMEM((T_PROJ + 8, D_CONV), f32)],
        compiler_params=cp(dimension_semantics=("arbitrary", "arbitrary")),
        name="proj",
    )(x, g_pre.reshape(1, D_MODEL), wm, wi, conv_w, c_t, sa_t, sb_t)

    nq_tiles = S // T_Q
    lag = lambda g: jnp.maximum(g - 1, 0)
    sel = lambda g: jnp.minimum(g, nq_tiles - 1)
    whole = lambda *shape: pl.BlockSpec((1,) + shape, lambda b, g: (b,) + (0,) * len(shape),
                                        pipeline_mode=pl.Buffered(1))
    ya = pl.pallas_call(
        functools.partial(_attn_kernel, top_k=top_k, n_tiles=nq_tiles),
        grid=(B, nq_tiles + 1),
        in_specs=[pl.BlockSpec((1, 1, N_HEADS, LANES, T_Q), lambda b, g: (b, lag(g), 0, 0, 0)),
                  pl.BlockSpec((1, 1, nq, T_Q), lambda b, g: (b, sel(g), 0, 0)),
                  pl.BlockSpec((1, 1, SUBLANES, T_Q), lambda b, g: (b, sel(g), 0, 0)),
                  pl.BlockSpec((1, T_Q, D_ATTN), lambda b, g: (b, lag(g), 0)),
                  whole(D_ATTN // LANES, S, LANES), whole(nc, N_HEADS, V_ROWS, S_C), whole(S, IDX_DIM)],
        out_specs=pl.BlockSpec((1, T_Q, D_ATTN), lambda b, g: (b, lag(g), 0)),
        out_shape=sds((B, S, D_ATTN), bf16),
        scratch_shapes=[pltpu.VMEM((2 * (nc + 1), S_C, T_Q), f32), pltpu.VMEM((N_HEADS, HEAD_DIM, T_Q), f32),
                        pltpu.VMEM((N_HEADS, SUBLANES, T_Q), f32), pltpu.VMEM((N_HEADS, SUBLANES, T_Q), f32),
                        pltpu.VMEM((N_HEADS, S_C, T_Q), bf16), pltpu.VMEM((N_HEADS, S_C, T_Q), bf16)],
        compiler_params=cp(dimension_semantics=("arbitrary", "arbitrary")),
        name="attn",
    )(qtz, qit, sgt, gz, k, vta, ki)

    otile = lambda w: pl.BlockSpec((1, T_OUT, w), lambda b, j: (b, j, 0))
    return pl.pallas_call(
        _out_kernel,
        grid=(B, S // T_OUT),
        in_specs=[otile(D_MODEL), otile(D_CONV), otile(D_ATTN), _const_spec((D_CONV, D_MODEL)),
                  _const_spec((D_ATTN, D_MODEL)), _const_spec((1, D_MODEL))],
        out_specs=otile(D_MODEL),
        out_shape=sds((B, S, D_MODEL), f32),
        compiler_params=cp(dimension_semantics=("arbitrary", "arbitrary")),
        name="outproj",
    )(x, yc, ya, woc, woa, g_post.reshape(1, D_MODEL))


def kernel(x, norm_pre_g, w_in, conv_w, w_out, norm_post_g):
    w_in, w_out = w_in.astype(bf16), w_out.astype(bf16)
    for l in range(norm_pre_g.shape[0]):
        x = _layer(x, norm_pre_g[l], w_in[l], conv_w[l], w_out[l], norm_post_g[l])
    return x
```

```python
import functools

import jax
import jax.numpy as jnp
from jax import lax
from jax.experimental import pallas as pl
from jax.experimental.pallas import tpu as pltpu

D_MODEL = 1024
CHUNK = 64
D_CONV = 512
CONV_WIDTH = 3
N_HEADS = 8
HEAD_DIM = 64
D_ATTN = N_HEADS * HEAD_DIM
ROT_DIM = HEAD_DIM // 4
ROPE_THETA = 500000.0
IDX_HEADS = 4
IDX_DIM = 64
INDEX_TOPK = 256
EPS = 1e-6

LANES = 128
SUBLANES = 8
T_PROJ = 512
T_Q = 256
S_C = 256
T_OUT = 1024
IDX_W = 512
N_BISECT = 22
NEG = -1e30
RIDE = 16
V_ROWS = HEAD_DIM + 16
Q_SCALE = HEAD_DIM ** -0.5 * 1.4426950408889634
VMEM_LIMIT = 56 * 1024 * 1024

f32 = jnp.float32
bf16 = jnp.bfloat16


def _rope128(xb, c, sa, sb):
    return xb * c + pltpu.roll(xb, LANES - ROT_DIM // 2, 1) * sa + pltpu.roll(xb, ROT_DIM // 2, 1) * sb


def _proj_kernel(x_ref, g_ref, wm_ref, wi_ref, cw_ref, c_ref, sa_ref, sb_ref,
                 yc_ref, qtz_ref, k_ref, vta_ref, gz_ref, qit_ref, ki_ref, sgt_ref, ubuf):
    j = pl.program_id(1)
    T = T_PROJ

    @pl.when(j == 0)
    def _():
        ubuf[0:8, :] = jnp.zeros((8, D_CONV), f32)

    x = x_ref[0]
    xn = x * lax.rsqrt(jnp.mean(x * x, axis=-1, keepdims=True) + EPS) * g_ref[...]
    xn = xn.astype(bf16)

    wc_ref, wa_ref = 0, 4

    def proj(first, k):
        return jnp.dot(xn, wm_ref[:, (first + k) * 512:(first + k + 1) * 512], preferred_element_type=f32)

    u = proj(wc_ref, 2) * proj(wc_ref, 0)
    ubuf[8:8 + T, :] = u
    conv = (ubuf[6:6 + T, :] * cw_ref[0:1, :] + ubuf[7:7 + T, :] * cw_ref[1:2, :] + u * cw_ref[2:3, :])
    ubuf[0:8, :] = ubuf[T:T + 8, :]
    zc = proj(wc_ref, 3)
    yc_ref[0] = (proj(wc_ref, 1) * conv * (zc * jax.nn.sigmoid(zc))).astype(bf16)

    c = c_ref[...]
    sa = sa_ref[...]
    sb = sb_ref[...]

    qf = proj(wa_ref, 0)
    kf = proj(wa_ref, 1)
    vf = proj(wa_ref, 2)
    sub = lax.broadcasted_iota(jnp.int32, (LANES, S_C), 0)
    chunks = [slice(r * S_C, (r + 1) * S_C) for r in range(T // S_C)]
    for i in range(D_ATTN // LANES):
        sl = slice(i * LANES, (i + 1) * LANES)
        qr = _rope128(qf[:, sl], c, sa, sb) * Q_SCALE
        k_ref[0, i] = _rope128(kf[:, sl], c, sa, sb).astype(bf16)
        for r, rs in enumerate(chunks):
            qt = qr[rs].T
            qtz_ref[0, r, 2 * i] = jnp.where(sub < HEAD_DIM, qt, 0.0).astype(bf16)
            qtz_ref[0, r, 2 * i + 1] = jnp.where(sub >= HEAD_DIM, qt, 0.0).astype(bf16)
            vt = vf[rs, sl].T.astype(bf16)
            for h in (2 * i, 2 * i + 1):
                vta_ref[0, r, h, 0:HEAD_DIM, :] = vt[(h % 2) * HEAD_DIM:(h % 2 + 1) * HEAD_DIM, :]
                vta_ref[0, r, h, HEAD_DIM:V_ROWS, :] = jnp.ones((V_ROWS - HEAD_DIM, S_C), bf16)
    za = proj(wa_ref, 3)
    gz_ref[0] = za * jax.nn.sigmoid(za)

    pi = jnp.dot(xn, wi_ref[...], preferred_element_type=f32)
    wv = pi[:, 3 * LANES:4 * LANES]
    sgn = jnp.where(wv > 0, 1.0, jnp.where(wv < 0, -1.0, 0.0))
    for r, rs in enumerate(chunks):
        sgt_ref[0, r] = sgn[rs].T[0:SUBLANES, :]
    wabs = jnp.abs(wv) * ((IDX_DIM ** -0.5) * (IDX_HEADS ** -0.5))
    lane = lax.broadcasted_iota(jnp.int32, (T, LANES), 1)
    for i in range(2):
        sl = slice(i * LANES, (i + 1) * LANES)
        scale = jnp.where(lane < IDX_DIM, wabs[:, 2 * i:2 * i + 1], wabs[:, 2 * i + 1:2 * i + 2])
        qi = _rope128(pi[:, sl], c, sa, sb) * scale
        for r, rs in enumerate(chunks):
            qit_ref[0, r, sl, :] = qi[rs].T.astype(bf16)
    ki_ref[0] = _rope128(pi[:, 2 * LANES:3 * LANES], c, sa, sb)[:, 0:IDX_DIM].astype(bf16)


def _attn_kernel(qtz_ref, qit_ref, sgt_ref, gz_ref, k_ref, vta_ref, ki_ref, o_ref,
                 sc_ref, acc_ref, m_ref, l_ref, a0_ref, a1_ref, *, top_k, n_tiles):
    g = pl.program_id(1)
    TQ, SC = T_Q, S_C
    G = SC // SUBLANES
    nslot = sc_ref.shape[0] // 2
    nc = nslot - 1
    i = jnp.minimum(g, n_tiles - 1)
    ip = g - 1
    live = g < n_tiles
    cur = (g % 2) * nslot
    prev = nslot - cur

    def fold(a, op):
        return op(a.reshape(G, SUBLANES, TQ), axis=0)

    def rows(j):
        return pl.ds(pl.multiple_of(j * SC, SC), SC)

    def trips(n):
        return jnp.where(live, n, 0)

    t_loc = lax.broadcasted_iota(jnp.int32, (1, TQ), 1)
    lim_local = ((t_loc // CHUNK) + 1) * CHUNK
    limit = lim_local + i * TQ
    keff = jnp.minimum(limit, top_k).astype(f32)
    adm_diag = lax.broadcasted_iota(jnp.int32, (SC, TQ), 0) < lim_local

    sgt = sgt_ref[0, 0]
    sg = [sgt[h:h + 1, :] for h in range(IDX_HEADS)]

    def score_chunk(j):
        kc = ki_ref[0, rows(j), :]
        s = None
        for h in range(IDX_HEADS):
            l = jnp.dot(kc, qit_ref[0, 0, h * IDX_DIM:(h + 1) * IDX_DIM, :], preferred_element_type=f32)
            t = jnp.maximum(l, 0.0) * sg[h]
            s = t if s is None else s + t
        return s

    minus_inf = jnp.full((SC, TQ), -jnp.inf, f32)
    sc_ref[cur + nc] = minus_inf
    sc_ref[prev + nc] = minus_inf

    def p1_chunks(js, carry):
        mn, mx = carry
        for j in js:
            s = score_chunk(j)
            sc_ref[cur + j] = s
            mn, mx = jnp.minimum(mn, fold(s, jnp.min)), jnp.maximum(mx, fold(s, jnp.max))
        return mn, mx

    mn0 = jnp.full((SUBLANES, TQ), jnp.inf, f32)
    mx0 = jnp.full((SUBLANES, TQ), -jnp.inf, f32)
    U = 4
    carry = lax.fori_loop(0, trips(i // U), lambda jj, c: p1_chunks([U * jj + q for q in range(U)], c), (mn0, mx0))
    mn, mx = lax.fori_loop(trips(i // U * U), trips(i), lambda j, c: p1_chunks([j], c), carry)
    s = score_chunk(i)
    sc_ref[cur + i] = jnp.where(adm_diag, s, -jnp.inf)
    mn = jnp.minimum(mn, fold(jnp.where(adm_diag, s, jnp.inf), jnp.min))
    mx = jnp.maximum(mx, fold(jnp.where(adm_diag, s, -jnp.inf), jnp.max))
    lo = jnp.min(mn, axis=0, keepdims=True)
    hi = jnp.max(mx, axis=0, keepdims=True)

    acc_ref[...] = jnp.zeros(acc_ref.shape, f32)
    m_ref[...] = jnp.full(m_ref.shape, NEG, f32)
    l_ref[...] = jnp.zeros(l_ref.shape, f32)
    P = 2 * SUBLANES

    def all_sublanes(x8, op):
        for shift in (4, 2, 1):
            x8 = op(x8, pltpu.roll(x8, shift, 0))
        return x8

    def kv_chunk(c):
        return jnp.clip(c, 0, jnp.maximum(ip, 0))

    def logits_to(buf, c, between=lambda h: None):
        bias = sc_ref[jnp.where(c <= ip, prev + c, prev + nc)]
        for h in range(N_HEADS):
            kc = k_ref[0, h // 2, rows(kv_chunk(c)), :]
            buf[h] = (jnp.dot(kc, qtz_ref[0, 0, h], preferred_element_type=f32) + bias).astype(bf16)
            between(h)

    def softmax_pv(buf, c, between=lambda h: None):
        for h in range(N_HEADS):
            a = buf[h]
            cmax = jnp.max(a.reshape(SC // P, P, TQ), axis=0).astype(f32)
            m_old = m_ref[h]
            m_new = jnp.maximum(m_old, all_sublanes(jnp.maximum(cmax[0:SUBLANES], cmax[SUBLANES:P]), jnp.maximum))
            alpha = jnp.exp2(m_old - m_new)
            m_ref[h] = m_new
            mb = jnp.concatenate([m_new, m_new], axis=0).astype(bf16)
            p = jnp.exp2(a.reshape(SC // P, P, TQ) - mb[None]).reshape(SC, TQ)
            pv = jnp.dot(vta_ref[0, kv_chunk(c), h], p, preferred_element_type=f32)
            l_ref[h] = alpha * l_ref[h] + pv[HEAD_DIM:HEAD_DIM + SUBLANES]
            acc = acc_ref[h].reshape(HEAD_DIM // SUBLANES, SUBLANES, TQ) * alpha[None]
            acc_ref[h] = acc.reshape(HEAD_DIM, TQ) + pv[0:HEAD_DIM]
            between(h)

    def chunk_count(slot, mid8):
        return jnp.sum(jnp.where(sc_ref[slot].reshape(G, SUBLANES, TQ) >= mid8[None], 1.0, 0.0), axis=0)

    def update(state, mid, c, take=True):
        lo, hi, clo = state
        ge = c >= keff
        up = jnp.logical_and(ge, take)
        down = jnp.logical_and(jnp.logical_not(ge), take)
        return jnp.where(up, mid, lo), jnp.where(down, mid, hi), jnp.where(up, c, clo)

    n = i + 1
    n_trips = (g + 1) // 2
    per_step = (n + RIDE - 1) // RIDE
    n_ride = jnp.where(live, jnp.minimum(n_trips // per_step, N_BISECT), 0)
    logits_to(a0_ref, 0)

    def p4_body(jj, carry):
        lo, hi, clo, cnt, cursor, done = carry
        j = 2 * jj
        active = done < n_ride
        mid = 0.5 * lo + 0.5 * hi
        mid8 = jnp.broadcast_to(mid, (SUBLANES, TQ))

        counts = [cnt]

        def ride(stage):
            per_stage = RIDE // 4

            def between(h):
                for t in range(h * per_stage // N_HEADS, (h + 1) * per_stage // N_HEADS):
                    c = cursor + stage * per_stage + t
                    slot = jnp.where(jnp.logical_and(c <= i, active), cur + c, cur + nc)
                    counts[0] = counts[0] + chunk_count(slot, mid8)
            return between

        logits_to(a1_ref, j + 1, ride(0))
        softmax_pv(a0_ref, j, ride(1))
        logits_to(a0_ref, j + 2, ride(2))
        softmax_pv(a1_ref, j + 1, ride(3))
        cnt = counts[0]
        cursor = cursor + RIDE
        full = jnp.logical_and(cursor >= n, active)
        lo, hi, clo = update((lo, hi, clo), mid, jnp.sum(cnt, axis=0, keepdims=True), take=full)
        keep = jnp.where(full, 0.0, 1.0)
        return lo, hi, clo, cnt * keep, jnp.where(full, 0, cursor), done + full.astype(jnp.int32)

    clo = limit.astype(f32)
    lo, hi, clo, _, _, done = lax.fori_loop(
        0, n_trips, p4_body, (lo, hi, clo, jnp.zeros((SUBLANES, TQ), f32), jnp.int32(0), jnp.int32(0)))

    @pl.when(g >= 1)
    def _():
        for hp in range(N_HEADS // 2):
            pair = jnp.concatenate(
                [(acc_ref[h].reshape(HEAD_DIM // SUBLANES, SUBLANES, TQ) / l_ref[h][None]).reshape(HEAD_DIM, TQ)
                 for h in (2 * hp, 2 * hp + 1)], axis=0)
            sl = slice(hp * LANES, (hp + 1) * LANES)
            o_ref[0, :, sl] = (pair.T * gz_ref[0, :, sl]).astype(bf16)

    def count_ge(mid):
        mid8 = jnp.broadcast_to(mid, (SUBLANES, TQ))

        def body(jj, cnt):
            j2 = 2 * jj + 1
            return (cnt + chunk_count(cur + 2 * jj, mid8)
                    + chunk_count(jnp.where(j2 <= i, cur + j2, cur + nc), mid8))
        cnt = lax.fori_loop(0, i // 2 + 1, body, jnp.zeros((SUBLANES, TQ), f32))
        return jnp.sum(cnt, axis=0, keepdims=True)

    def bisect(state):
        mid = 0.5 * state[0] + 0.5 * state[1]
        return update(state, mid, count_ge(mid))

    def snap(lo, hi):
        def body(jj, carry):
            a, b = carry
            for j in (2 * jj, jnp.minimum(2 * jj + 1, i)):
                s = sc_ref[cur + j]
                a = jnp.minimum(a, fold(jnp.where(s >= lo, s, jnp.inf), jnp.min))
                b = jnp.maximum(b, fold(jnp.where(s <= hi, s, -jnp.inf), jnp.max))
            return a, b
        a, b = lax.fori_loop(0, trips(i // 2 + 1), body, (mn0, mx0))
        return jnp.min(a, axis=0, keepdims=True), jnp.max(b, axis=0, keepdims=True)

    lo, hi, clo = lax.fori_loop(0, trips(N_BISECT - done), lambda _, st: bisect(st), (lo, hi, clo))
    vlo, vhi = snap(lo, hi)

    def unconverged(lo, hi, clo, vlo, vhi):
        bad = jnp.logical_and(clo != keff, vlo != vhi)
        return jnp.logical_and(live, jnp.max(jnp.where(bad, 1.0, 0.0)) > 0.0)

    def refine(state):
        lo, hi, clo, _, _ = state
        lo, hi, clo = bisect((lo, hi, clo))
        vlo, vhi = snap(lo, hi)
        return lo, hi, clo, vlo, vhi

    lo, hi, clo, vlo, vhi = lax.while_loop(lambda st: unconverged(*st), refine, (lo, hi, clo, vlo, vhi))
    tau = vlo
    excess = clo - keff

    ka = lax.broadcasted_iota(jnp.int32, (SC, SC), 0)
    kb = lax.broadcasted_iota(jnp.int32, (SC, SC), 1)
    upper = jnp.where(kb > ka, 1.0, 0.0).astype(bf16)

    def p3_chunks(slots, carry):
        loaded = [(slot, sc_ref[slot]) for slot in slots]
        for slot, s in loaded:
            eq = s == tau
            eqf = jnp.where(eq, 1.0, 0.0)
            later = jnp.dot(upper, eqf.astype(bf16), preferred_element_type=f32) + carry
            sel = jnp.logical_or(s > tau, jnp.logical_and(eq, later >= excess))
            sc_ref[slot] = jnp.where(sel, 0.0, NEG)
            carry = carry + jnp.sum(fold(eqf, jnp.sum), axis=0, keepdims=True)
        return carry

    rest = n % U
    carry = lax.fori_loop(0, trips(rest), lambda r, c: p3_chunks([cur + i - r], c), jnp.zeros((1, TQ), f32))
    lax.fori_loop(0, trips(n // U),
                  lambda jj, c: p3_chunks([cur + i - rest - U * jj - q for q in range(U)], c), carry)


def _out_kernel(x_ref, yc_ref, ya_ref, woc_ref, woa_ref, g_ref, o_ref):
    y = (jnp.dot(yc_ref[0], woc_ref[...], preferred_element_type=f32)
         + jnp.dot(ya_ref[0], woa_ref[...], preferred_element_type=f32))
    yn = y * lax.rsqrt(jnp.mean(y * y, axis=-1, keepdims=True) + EPS) * g_ref[...]
    o_ref[0] = x_ref[0] + yn


def _rope_tables(S):
    half = ROT_DIM // 2
    inv_freq = jnp.power(jnp.float32(ROPE_THETA), -jnp.arange(half, dtype=f32) / half)
    ang = jnp.arange(S, dtype=jnp.int32).astype(f32)[:, None] * inv_freq[None, :]
    cos, sin = jnp.cos(ang), jnp.sin(ang)
    ones = jnp.ones((S, HEAD_DIM - ROT_DIM), f32)
    zeros = jnp.zeros((S, HEAD_DIM - ROT_DIM), f32)
    zh = jnp.zeros((S, half), f32)
    c = jnp.concatenate([cos, cos, ones], axis=1)
    sa = jnp.concatenate([-sin, zh, zeros], axis=1)
    sb = jnp.concatenate([zh, sin, zeros], axis=1)
    rep = LANES // HEAD_DIM
    return jnp.tile(c, (1, rep)), jnp.tile(sa, (1, rep)), jnp.tile(sb, (1, rep))


def _const_spec(shape):
    nd = len(shape)
    return pl.BlockSpec(shape, lambda b, j: (0,) * nd, pipeline_mode=pl.Buffered(1))


def _layer(x, g_pre, w_in, conv_w, w_out, g_post):
    B, S, _ = x.shape
    assert S % T_PROJ == 0 and S % T_Q == 0 and S % T_OUT == 0 and T_Q == S_C and T_PROJ % S_C == 0
    top_k = min(INDEX_TOPK, S // 4)
    nc = S // S_C

    o = 4 * D_CONV + 4 * D_ATTN
    wm = w_in[:, 0:o]
    nq = IDX_HEADS * IDX_DIM
    wi = jnp.zeros((D_MODEL, IDX_W), bf16)
    wi = wi.at[:, 0:nq + IDX_DIM].set(w_in[:, o:o + nq + IDX_DIM])
    wi = wi.at[:, 3 * LANES:3 * LANES + IDX_HEADS].set(w_in[:, o + nq + IDX_DIM:o + nq + IDX_DIM + IDX_HEADS])
    woc = w_out[0:D_CONV]
    woa = w_out[D_CONV:]
    c_t, sa_t, sb_t = _rope_tables(S)

    cp = functools.partial(pltpu.CompilerParams, vmem_limit_bytes=VMEM_LIMIT)

    tile = lambda w: pl.BlockSpec((1, T_PROJ, w), lambda b, j: (b, j, 0))
    ttile = lambda *shape: pl.BlockSpec((1, T_PROJ // S_C) + shape, lambda b, j: (b, j) + (0,) * len(shape))
    tab = pl.BlockSpec((T_PROJ, LANES), lambda b, j: (j, 0))
    sds = jax.ShapeDtypeStruct
    yc, qtz, k, vta, gz, qit, ki, sgt = pl.pallas_call(
        _proj_kernel,
        grid=(B, S // T_PROJ),
        in_specs=[tile(D_MODEL), _const_spec((1, D_MODEL)), _const_spec((D_MODEL, 4 * D_CONV + 4 * D_ATTN)),
                  _const_spec((D_MODEL, IDX_W)),
                  _const_spec((CONV_WIDTH, D_CONV)), tab, tab, tab],
        out_specs=[tile(D_CONV), ttile(N_HEADS, LANES, S_C),
                   pl.BlockSpec((1, D_ATTN // LANES, T_PROJ, LANES), lambda b, j: (b, 0, j, 0)),
                   ttile(N_HEADS, V_ROWS, S_C),
                   tile(D_ATTN), ttile(nq, S_C), tile(IDX_DIM), ttile(SUBLANES, S_C)],
        out_shape=[sds((B, S, D_CONV), bf16), sds((B, nc, N_HEADS, LANES, S_C), bf16),
                   sds((B, D_ATTN // LANES, S, LANES), bf16), sds((B, nc, N_HEADS, V_ROWS, S_C), bf16),
                   sds((B, S, D_ATTN), f32), sds((B, nc, nq, S_C), bf16),
                   sds((B, S, IDX_DIM), bf16), sds((B, nc, SUBLANES, S_C), f32)],
        scratch_shapes=[pltpu.VMEM((T_PROJ + 8, D_CONV), f32)],
        compiler_params=cp(dimension_semantics=("arbitrary", "arbitrary")),
        name="proj",
    )(x, g_pre.reshape(1, D_MODEL), wm, wi, conv_w, c_t, sa_t, sb_t)

    nq_tiles = S // T_Q
    lag = lambda g: jnp.maximum(g - 1, 0)
    sel = lambda g: jnp.minimum(g, nq_tiles - 1)
    whole = lambda *shape: pl.BlockSpec((1,) + shape, lambda b, g: (b,) + (0,) * len(shape),
                                        pipeline_mode=pl.Buffered(1))
    ya = pl.pallas_call(
        functools.partial(_attn_kernel, top_k=top_k, n_tiles=nq_tiles),
        grid=(B, nq_tiles + 1),
        in_specs=[pl.BlockSpec((1, 1, N_HEADS, LANES, T_Q), lambda b, g: (b, lag(g), 0, 0, 0)),
                  pl.BlockSpec((1, 1, nq, T_Q), lambda b, g: (b, sel(g), 0, 0)),
                  pl.BlockSpec((1, 1, SUBLANES, T_Q), lambda b, g: (b, sel(g), 0, 0)),
                  pl.BlockSpec((1, T_Q, D_ATTN), lambda b, g: (b, lag(g), 0)),
                  whole(D_ATTN // LANES, S, LANES), whole(nc, N_HEADS, V_ROWS, S_C), whole(S, IDX_DIM)],
        out_specs=pl.BlockSpec((1, T_Q, D_ATTN), lambda b, g: (b, lag(g), 0)),
        out_shape=sds((B, S, D_ATTN), bf16),
        scratch_shapes=[pltpu.VMEM((2 * (nc + 1), S_C, T_Q), f32), pltpu.VMEM((N_HEADS, HEAD_DIM, T_Q), f32),
                        pltpu.VMEM((N_HEADS, SUBLANES, T_Q), f32), pltpu.VMEM((N_HEADS, SUBLANES, T_Q), f32),
                        pltpu.VMEM((N_HEADS, S_C, T_Q), bf16), pltpu.VMEM((N_HEADS, S_C, T_Q), bf16)],
        compiler_params=cp(dimension_semantics=("arbitrary", "arbitrary")),
        name="attn",
    )(qtz, qit, sgt, gz, k, vta, ki)

    otile = lambda w: pl.BlockSpec((1, T_OUT, w), lambda b, j: (b, j, 0))
    return pl.pallas_call(
        _out_kernel,
        grid=(B, S // T_OUT),
        in_specs=[otile(D_MODEL), otile(D_CONV), otile(D_ATTN), _const_spec((D_CONV, D_MODEL)),
                  _const_spec((D_ATTN, D_MODEL)), _const_spec((1, D_MODEL))],
        out_specs=otile(D_MODEL),
        out_shape=sds((B, S, D_MODEL), f32),
        compiler_params=cp(dimension_semantics=("arbitrary", "arbitrary")),
        name="outproj",
    )(x, yc, ya, woc, woa, g_post.reshape(1, D_MODEL))


def kernel(x, norm_pre_g, w_in, conv_w, w_out, norm_post_g):
    w_in, w_out = w_in.astype(bf16), w_out.astype(bf16)
    for l in range(norm_pre_g.shape[0]):
        x = _layer(x, norm_pre_g[l], w_in[l], conv_w[l], w_out[l], norm_post_g[l])
    return x
```

```python
import functools

import jax
import jax.numpy as jnp
from jax import lax
from jax.experimental import pallas as pl
from jax.experimental.pallas import tpu as pltpu

D_MODEL = 1024
CHUNK = 64
D_CONV = 512
CONV_WIDTH = 3
N_HEADS = 8
HEAD_DIM = 64
D_ATTN = N_HEADS * HEAD_DIM
ROT_DIM = HEAD_DIM // 4
ROPE_THETA = 500000.0
IDX_HEADS = 4
IDX_DIM = 64
INDEX_TOPK = 256
EPS = 1e-6

LANES = 128
SUBLANES = 8
T_PROJ = 512
T_Q = 256
S_C = 256
T_OUT = 1024
IDX_W = 512
N_BISECT = 22
NEG = -1e30
RIDE = 16
V_ROWS = HEAD_DIM + 16
Q_SCALE = HEAD_DIM ** -0.5 * 1.4426950408889634
VMEM_LIMIT = 56 * 1024 * 1024

f32 = jnp.float32
bf16 = jnp.bfloat16


def _rope128(xb, c, sa, sb):
    return xb * c + pltpu.roll(xb, LANES - ROT_DIM // 2, 1) * sa + pltpu.roll(xb, ROT_DIM // 2, 1) * sb


def _proj_kernel(x_ref, g_ref, wm_ref, wi_ref, cw_ref, c_ref, sa_ref, sb_ref,
                 yc_ref, qtz_ref, k_ref, vta_ref, gz_ref, qit_ref, ki_ref, sgt_ref, ubuf):
    j = pl.program_id(1)
    T = T_PROJ

    @pl.when(j == 0)
    def _():
        ubuf[0:8, :] = jnp.zeros((8, D_CONV), f32)

    x = x_ref[0]
    xn = x * lax.rsqrt(jnp.mean(x * x, axis=-1, keepdims=True) + EPS) * g_ref[...]
    xn = xn.astype(bf16)

    wc_ref, wa_ref = 0, 4

    def proj(first, k):
        return jnp.dot(xn, wm_ref[:, (first + k) * 512:(first + k + 1) * 512], preferred_element_type=f32)

    u = proj(wc_ref, 2) * proj(wc_ref, 0)
    ubuf[8:8 + T, :] = u
    conv = (ubuf[6:6 + T, :] * cw_ref[0:1, :] + ubuf[7:7 + T, :] * cw_ref[1:2, :] + u * cw_ref[2:3, :])
    ubuf[0:8, :] = ubuf[T:T + 8, :]
    zc = proj(wc_ref, 3)
    yc_ref[0] = (proj(wc_ref, 1) * conv * (zc * jax.nn.sigmoid(zc))).astype(bf16)

    c = c_ref[...]
    sa = sa_ref[...]
    sb = sb_ref[...]

    qf = proj(wa_ref, 0)
    kf = proj(wa_ref, 1)
    vf = proj(wa_ref, 2)
    sub = lax.broadcasted_iota(jnp.int32, (LANES, S_C), 0)
    chunks = [slice(r * S_C, (r + 1) * S_C) for r in range(T // S_C)]
    for i in range(D_ATTN // LANES):
        sl = slice(i * LANES, (i + 1) * LANES)
        qr = _rope128(qf[:, sl], c, sa, sb) * Q_SCALE
        k_ref[0, i] = _rope128(kf[:, sl], c, sa, sb).astype(bf16)
        for r, rs in enumerate(chunks):
            qt = qr[rs].T
            qtz_ref[0, r, 2 * i] = jnp.where(sub < HEAD_DIM, qt, 0.0).astype(bf16)
            qtz_ref[0, r, 2 * i + 1] = jnp.where(sub >= HEAD_DIM, qt, 0.0).astype(bf16)
            vt = vf[rs, sl].T.astype(bf16)
            for h in (2 * i, 2 * i + 1):
                vta_ref[0, r, h, 0:HEAD_DIM, :] = vt[(h % 2) * HEAD_DIM:(h % 2 + 1) * HEAD_DIM, :]
                vta_ref[0, r, h, HEAD_DIM:V_ROWS, :] = jnp.ones((V_ROWS - HEAD_DIM, S_C), bf16)
    za = proj(wa_ref, 3)
    gz_ref[0] = za * jax.nn.sigmoid(za)

    pi = jnp.dot(xn, wi_ref[...], preferred_element_type=f32)
    wv = pi[:, 3 * LANES:4 * LANES]
    sgn = jnp.where(wv > 0, 1.0, jnp.where(wv < 0, -1.0, 0.0))
    for r, rs in enumerate(chunks):
        sgt_ref[0, r] = sgn[rs].T[0:SUBLANES, :]
    wabs = jnp.abs(wv) * ((IDX_DIM ** -0.5) * (IDX_HEADS ** -0.5))
    lane = lax.broadcasted_iota(jnp.int32, (T, LANES), 1)
    for i in range(2):
        sl = slice(i * LANES, (i + 1) * LANES)
        scale = jnp.where(lane < IDX_DIM, wabs[:, 2 * i:2 * i + 1], wabs[:, 2 * i + 1:2 * i + 2])
        qi = _rope128(pi[:, sl], c, sa, sb) * scale
        for r, rs in enumerate(chunks):
            qit_ref[0, r, sl, :] = qi[rs].T.astype(bf16)
    ki_ref[0] = _rope128(pi[:, 2 * LANES:3 * LANES], c, sa, sb)[:, 0:IDX_DIM].astype(bf16)


def _attn_kernel(qtz_ref, qit_ref, sgt_ref, gz_ref, k_ref, vta_ref, ki_ref, o_ref,
                 sc_ref, acc_ref, m_ref, l_ref, a0_ref, a1_ref, *, top_k, n_tiles):
    g = pl.program_id(1)
    TQ, SC = T_Q, S_C
    G = SC // SUBLANES
    nslot = sc_ref.shape[0] // 2
    nc = nslot - 1
    i = jnp.minimum(g, n_tiles - 1)
    ip = g - 1
    live = g < n_tiles
    cur = (g % 2) * nslot
    prev = nslot - cur

    def fold(a, op):
        return op(a.reshape(G, SUBLANES, TQ), axis=0)

    def rows(j):
        return pl.ds(pl.multiple_of(j * SC, SC), SC)

    def trips(n):
        return jnp.where(live, n, 0)

    t_loc = lax.broadcasted_iota(jnp.int32, (1, TQ), 1)
    lim_local = ((t_loc // CHUNK) + 1) * CHUNK
    limit = lim_local + i * TQ
    keff = jnp.minimum(limit, top_k).astype(f32)
    adm_diag = lax.broadcasted_iota(jnp.int32, (SC, TQ), 0) < lim_local

    sgt = sgt_ref[0, 0]
    sg = [sgt[h:h + 1, :] for h in range(IDX_HEADS)]

    def score_chunk(j):
        kc = ki_ref[0, rows(j), :]
        s = None
        for h in range(IDX_HEADS):
            l = jnp.dot(kc, qit_ref[0, 0, h * IDX_DIM:(h + 1) * IDX_DIM, :], preferred_element_type=f32)
            t = jnp.maximum(l, 0.0) * sg[h]
            s = t if s is None else s + t
        return s

    minus_inf = jnp.full((SC, TQ), -jnp.inf, f32)
    sc_ref[cur + nc] = minus_inf
    sc_ref[prev + nc] = minus_inf

    def p1_chunks(js, carry):
        mn, mx = carry
        for j in js:
            s = score_chunk(j)
            sc_ref[cur + j] = s
            mn, mx = jnp.minimum(mn, fold(s, jnp.min)), jnp.maximum(mx, fold(s, jnp.max))
        return mn, mx

    mn0 = jnp.full((SUBLANES, TQ), jnp.inf, f32)
    mx0 = jnp.full((SUBLANES, TQ), -jnp.inf, f32)
    U = 4
    carry = lax.fori_loop(0, trips(i // U), lambda jj, c: p1_chunks([U * jj + q for q in range(U)], c), (mn0, mx0))
    mn, mx = lax.fori_loop(trips(i // U * U), trips(i), lambda j, c: p1_chunks([j], c), carry)
    s = score_chunk(i)
    sc_ref[cur + i] = jnp.where(adm_diag, s, -jnp.inf)
    mn = jnp.minimum(mn, fold(jnp.where(adm_diag, s, jnp.inf), jnp.min))
    mx = jnp.maximum(mx, fold(jnp.where(adm_diag, s, -jnp.inf), jnp.max))
    lo = jnp.min(mn, axis=0, keepdims=True)
    hi = jnp.max(mx, axis=0, keepdims=True)

    acc_ref[...] = jnp.zeros(acc_ref.shape, f32)
    m_ref[...] = jnp.full(m_ref.shape, NEG, f32)
    l_ref[...] = jnp.zeros(l_ref.shape, f32)
    P = 2 * SUBLANES

    def all_sublanes(x8, op):
        for shift in (4, 2, 1):
            x8 = op(x8, pltpu.roll(x8, shift, 0))
        return x8

    def kv_chunk(c):
        return jnp.clip(c, 0, jnp.maximum(ip, 0))

    def logits_to(buf, c, between=lambda h: None):
        bias = sc_ref[jnp.where(c <= ip, prev + c, prev + nc)]
        for h in range(N_HEADS):
            kc = k_ref[0, h // 2, rows(kv_chunk(c)), :]
            buf[h] = (jnp.dot(kc, qtz_ref[0, 0, h], preferred_element_type=f32) + bias).astype(bf16)
            between(h)

    def softmax_pv(buf, c, between=lambda h: None):
        for h in range(N_HEADS):
            a = buf[h]
            cmax = jnp.max(a.reshape(SC // P, P, TQ), axis=0).astype(f32)
            m_old = m_ref[h]
            m_new = jnp.maximum(m_old, all_sublanes(jnp.maximum(cmax[0:SUBLANES], cmax[SUBLANES:P]), jnp.maximum))
            alpha = jnp.exp2(m_old - m_new)
            m_ref[h] = m_new
            mb = jnp.concatenate([m_new, m_new], axis=0).astype(bf16)
            p = jnp.exp2(a.reshape(SC // P, P, TQ) - mb[None]).reshape(SC, TQ)
            pv = jnp.dot(vta_ref[0, kv_chunk(c), h], p, preferred_element_type=f32)
            l_ref[h] = alpha * l_ref[h] + pv[HEAD_DIM:HEAD_DIM + SUBLANES]
            acc = acc_ref[h].reshape(HEAD_DIM // SUBLANES, SUBLANES, TQ) * alpha[None]
            acc_ref[h] = acc.reshape(HEAD_DIM, TQ) + pv[0:HEAD_DIM]
            between(h)

    def chunk_count(slot, mid8):
        return jnp.sum(jnp.where(sc_ref[slot].reshape(G, SUBLANES, TQ) >= mid8[None], 1.0, 0.0), axis=0)

    def update(state, mid, c, take=True):
        lo, hi, clo = state
        ge = c >= keff
        up = jnp.logical_and(ge, take)
        down = jnp.logical_and(jnp.logical_not(ge), take)
        return jnp.where(up, mid, lo), jnp.where(down, mid, hi), jnp.where(up, c, clo)

    n = i + 1
    n_trips = (g + 1) // 2
    per_step = (n + RIDE - 1) // RIDE
    n_ride = jnp.where(live, jnp.minimum(n_trips // per_step, N_BISECT), 0)
    logits_to(a0_ref, 0)

    def p4_body(jj, carry):
        lo, hi, clo, cnt, cursor, done = carry
        j = 2 * jj
        active = done < n_ride
        mid = 0.5 * lo + 0.5 * hi
        mid8 = jnp.broadcast_to(mid, (SUBLANES, TQ))

        counts = [cnt]

        def ride(stage):
            per_stage = RIDE // 4

            def between(h):
                for t in range(h * per_stage // N_HEADS, (h + 1) * per_stage // N_HEADS):
                    c = cursor + stage * per_stage + t
                    slot = jnp.where(jnp.logical_and(c <= i, active), cur + c, cur + nc)
                    counts[0] = counts[0] + chunk_count(slot, mid8)
            return between

        logits_to(a1_ref, j + 1, ride(0))
        softmax_pv(a0_ref, j, ride(1))
        logits_to(a0_ref, j + 2, ride(2))
        softmax_pv(a1_ref, j + 1, ride(3))
        cnt = counts[0]
        cursor = cursor + RIDE
        full = jnp.logical_and(cursor >= n, active)
        lo, hi, clo = update((lo, hi, clo), mid, jnp.sum(cnt, axis=0, keepdims=True), take=full)
        keep = jnp.where(full, 0.0, 1.0)
        return lo, hi, clo, cnt * keep, jnp.where(full, 0, cursor), done + full.astype(jnp.int32)

    clo = limit.astype(f32)
    lo, hi, clo, _, _, done = lax.fori_loop(
        0, n_trips, p4_body, (lo, hi, clo, jnp.zeros((SUBLANES, TQ), f32), jnp.int32(0), jnp.int32(0)))

    @pl.when(g >= 1)
    def _():
        for hp in range(N_HEADS // 2):
            pair = jnp.concatenate(
                [(acc_ref[h].reshape(HEAD_DIM // SUBLANES, SUBLANES, TQ) / l_ref[h][None]).reshape(HEAD_DIM, TQ)
                 for h in (2 * hp, 2 * hp + 1)], axis=0)
            sl = slice(hp * LANES, (hp + 1) * LANES)
            o_ref[0, :, sl] = (pair.T * gz_ref[0, :, sl]).astype(bf16)

    def count_ge(mid):
        mid8 = jnp.broadcast_to(mid, (SUBLANES, TQ))

        def some(js, cnt):
            for j in js:
                cnt = cnt + chunk_count(cur + j, mid8)
            return cnt
        cnt = lax.fori_loop(0, n // U, lambda jj, c: some([U * jj + q for q in range(U)], c),
                            jnp.zeros((SUBLANES, TQ), f32))
        cnt = lax.fori_loop(n // U * U, n, lambda j, c: some([j], c), cnt)
        return jnp.sum(cnt, axis=0, keepdims=True)

    def bisect(state):
        mid = 0.5 * state[0] + 0.5 * state[1]
        return update(state, mid, count_ge(mid))

    def snap(lo, hi):
        def body(jj, carry):
            a, b = carry
            for j in (2 * jj, jnp.minimum(2 * jj + 1, i)):
                s = sc_ref[cur + j]
                a = jnp.minimum(a, fold(jnp.where(s >= lo, s, jnp.inf), jnp.min))
                b = jnp.maximum(b, fold(jnp.where(s <= hi, s, -jnp.inf), jnp.max))
            return a, b
        a, b = lax.fori_loop(0, trips(i // 2 + 1), body, (mn0, mx0))
        return jnp.min(a, axis=0, keepdims=True), jnp.max(b, axis=0, keepdims=True)

    lo, hi, clo = lax.fori_loop(0, trips(N_BISECT - done), lambda _, st: bisect(st), (lo, hi, clo))
    vlo, vhi = snap(lo, hi)

    def unconverged(lo, hi, clo, vlo, vhi):
        bad = jnp.logical_and(clo != keff, vlo != vhi)
        return jnp.logical_and(live, jnp.max(jnp.where(bad, 1.0, 0.0)) > 0.0)

    def refine(state):
        lo, hi, clo, _, _ = state
        lo, hi, clo = bisect((lo, hi, clo))
        vlo, vhi = snap(lo, hi)
        return lo, hi, clo, vlo, vhi

    lo, hi, clo, vlo, vhi = lax.while_loop(lambda st: unconverged(*st), refine, (lo, hi, clo, vlo, vhi))
    tau = vlo
    excess = clo - keff

    ka = lax.broadcasted_iota(jnp.int32, (SC, SC), 0)
    kb = lax.broadcasted_iota(jnp.int32, (SC, SC), 1)
    upper = jnp.where(kb > ka, 1.0, 0.0).astype(bf16)

    def p3_chunks(slots, carry):
        loaded = [(slot, sc_ref[slot]) for slot in slots]
        for slot, s in loaded:
            eq = s == tau
            eqf = jnp.where(eq, 1.0, 0.0)
            later = jnp.dot(upper, eqf.astype(bf16), preferred_element_type=f32) + carry
            sel = jnp.logical_or(s > tau, jnp.logical_and(eq, later >= excess))
            sc_ref[slot] = jnp.where(sel, 0.0, NEG)
            carry = carry + jnp.sum(fold(eqf, jnp.sum), axis=0, keepdims=True)
        return carry

    rest = n % U
    carry = lax.fori_loop(0, trips(rest), lambda r, c: p3_chunks([cur + i - r], c), jnp.zeros((1, TQ), f32))
    lax.fori_loop(0, trips(n // U),
                  lambda jj, c: p3_chunks([cur + i - rest - U * jj - q for q in range(U)], c), carry)


def _out_kernel(x_ref, yc_ref, ya_ref, woc_ref, woa_ref, g_ref, o_ref):
    y = (jnp.dot(yc_ref[0], woc_ref[...], preferred_element_type=f32)
         + jnp.dot(ya_ref[0], woa_ref[...], preferred_element_type=f32))
    yn = y * lax.rsqrt(jnp.mean(y * y, axis=-1, keepdims=True) + EPS) * g_ref[...]
    o_ref[0] = x_ref[0] + yn


def _rope_tables(S):
    half = ROT_DIM // 2
    inv_freq = jnp.power(jnp.float32(ROPE_THETA), -jnp.arange(half, dtype=f32) / half)
    ang = jnp.arange(S, dtype=jnp.int32).astype(f32)[:, None] * inv_freq[None, :]
    cos, sin = jnp.cos(ang), jnp.sin(ang)
    ones = jnp.ones((S, HEAD_DIM - ROT_DIM), f32)
    zeros = jnp.zeros((S, HEAD_DIM - ROT_DIM), f32)
    zh = jnp.zeros((S, half), f32)
    c = jnp.concatenate([cos, cos, ones], axis=1)
    sa = jnp.concatenate([-sin, zh, zeros], axis=1)
    sb = jnp.concatenate([zh, sin, zeros], axis=1)
    rep = LANES // HEAD_DIM
    return jnp.tile(c, (1, rep)), jnp.tile(sa, (1, rep)), jnp.tile(sb, (1, rep))


def _const_spec(shape):
    nd = len(shape)
    return pl.BlockSpec(shape, lambda b, j: (0,) * nd, pipeline_mode=pl.Buffered(1))


def _layer(x, g_pre, w_in, conv_w, w_out, g_post):
    B, S, _ = x.shape
    assert S % T_PROJ == 0 and S % T_Q == 0 and S % T_OUT == 0 and T_Q == S_C and T_PROJ % S_C == 0
    top_k = min(INDEX_TOPK, S // 4)
    nc = S // S_C

    o = 4 * D_CONV + 4 * D_ATTN
    wm = w_in[:, 0:o]
    nq = IDX_HEADS * IDX_DIM
    wi = jnp.zeros((D_MODEL, IDX_W), bf16)
    wi = wi.at[:, 0:nq + IDX_DIM].set(w_in[:, o:o + nq + IDX_DIM])
    wi = wi.at[:, 3 * LANES:3 * LANES + IDX_HEADS].set(w_in[:, o + nq + IDX_DIM:o + nq + IDX_DIM + IDX_HEADS])
    woc = w_out[0:D_CONV]
    woa = w_out[D_CONV:]
    c_t, sa_t, sb_t = _rope_tables(S)

    cp = functools.partial(pltpu.CompilerParams, vmem_limit_bytes=VMEM_LIMIT)

    tile = lambda w: pl.BlockSpec((1, T_PROJ, w), lambda b, j: (b, j, 0))
    ttile = lambda *shape: pl.BlockSpec((1, T_PROJ // S_C) + shape, lambda b, j: (b, j) + (0,) * len(shape))
    tab = pl.BlockSpec((T_PROJ, LANES), lambda b, j: (j, 0))
    sds = jax.ShapeDtypeStruct
    yc, qtz, k, vta, gz, qit, ki, sgt = pl.pallas_call(
        _proj_kernel,
        grid=(B, S // T_PROJ),
        in_specs=[tile(D_MODEL), _const_spec((1, D_MODEL)), _const_spec((D_MODEL, 4 * D_CONV + 4 * D_ATTN)),
                  _const_spec((D_MODEL, IDX_W)),
                  _const_spec((CONV_WIDTH, D_CONV)), tab, tab, tab],
        out_specs=[tile(D_CONV), ttile(N_HEADS, LANES, S_C),
                   pl.BlockSpec((1, D_ATTN // LANES, T_PROJ, LANES), lambda b, j: (b, 0, j, 0)),
                   ttile(N_HEADS, V_ROWS, S_C),
                   tile(D_ATTN), ttile(nq, S_C), tile(IDX_DIM), ttile(SUBLANES, S_C)],
        out_shape=[sds((B, S, D_CONV), bf16), sds((B, nc, N_HEADS, LANES, S_C), bf16),
                   sds((B, D_ATTN // LANES, S, LANES), bf16), sds((B, nc, N_HEADS, V_ROWS, S_C), bf16),
                   sds((B, S, D_ATTN), f32), sds((B, nc, nq, S_C), bf16),
                   sds((B, S, IDX_DIM), bf16), sds((B, nc, SUBLANES, S_C), f32)],
        scratch_shapes=[pltpu.VMEM((T_PROJ + 8, D_CONV), f32)],
        compiler_params=cp(dimension_semantics=("arbitrary", "arbitrary")),
        name="proj",
    )(x, g_pre.reshape(1, D_MODEL), wm, wi, conv_w, c_t, sa_t, sb_t)

    nq_tiles = S // T_Q
    lag = lambda g: jnp.maximum(g - 1, 0)
    sel = lambda g: jnp.minimum(g, nq_tiles - 1)
    whole = lambda *shape: pl.BlockSpec((1,) + shape, lambda b, g: (b,) + (0,) * len(shape),
                                        pipeline_mode=pl.Buffered(1))
    ya = pl.pallas_call(
        functools.partial(_attn_kernel, top_k=top_k, n_tiles=nq_tiles),
        grid=(B, nq_tiles + 1),
        in_specs=[pl.BlockSpec((1, 1, N_HEADS, LANES, T_Q), lambda b, g: (b, lag(g), 0, 0, 0)),
                  pl.BlockSpec((1, 1, nq, T_Q), lambda b, g: (b, sel(g), 0, 0)),
                  pl.BlockSpec((1, 1, SUBLANES, T_Q), lambda b, g: (b, sel(g), 0, 0)),
                  pl.BlockSpec((1, T_Q, D_ATTN), lambda b, g: (b, lag(g), 0)),
                  whole(D_ATTN // LANES, S, LANES), whole(nc, N_HEADS, V_ROWS, S_C), whole(S, IDX_DIM)],
        out_specs=pl.BlockSpec((1, T_Q, D_ATTN), lambda b, g: (b, lag(g), 0)),
        out_shape=sds((B, S, D_ATTN), bf16),
        scratch_shapes=[pltpu.VMEM((2 * (nc + 1), S_C, T_Q), f32), pltpu.VMEM((N_HEADS, HEAD_DIM, T_Q), f32),
                        pltpu.VMEM((N_HEADS, SUBLANES, T_Q), f32), pltpu.VMEM((N_HEADS, SUBLANES, T_Q), f32),
                        pltpu.VMEM((N_HEADS, S_C, T_Q), bf16), pltpu.VMEM((N_HEADS, S_C, T_Q), bf16)],
        compiler_params=cp(dimension_semantics=("arbitrary", "arbitrary")),
        name="attn",
    )(qtz, qit, sgt, gz, k, vta, ki)

    otile = lambda w: pl.BlockSpec((1, T_OUT, w), lambda b, j: (b, j, 0))
    return pl.pallas_call(
        _out_kernel,
        grid=(B, S // T_OUT),
        in_specs=[otile(D_MODEL), otile(D_CONV), otile(D_ATTN), _const_spec((D_CONV, D_MODEL)),
                  _const_spec((D_ATTN, D_MODEL)), _const_spec((1, D_MODEL))],
        out_specs=otile(D_MODEL),
        out_shape=sds((B, S, D_MODEL), f32),
        compiler_params=cp(dimension_semantics=("arbitrary", "arbitrary")),
        name="outproj",
    )(x, yc, ya, woc, woa, g_post.reshape(1, D_MODEL))


def kernel(x, norm_pre_g, w_in, conv_w, w_out, norm_post_g):
    w_in, w_out = w_in.astype(bf16), w_out.astype(bf16)
    for l in range(norm_pre_g.shape[0]):
        x = _layer(x, norm_pre_g[l], w_in[l], conv_w[l], w_out[l], norm_post_g[l])
    return x
```

```python
import functools

import jax
import jax.numpy as jnp
from jax import lax
from jax.experimental import pallas as pl
from jax.experimental.pallas import tpu as pltpu

D_MODEL = 1024
CHUNK = 64
D_CONV = 512
CONV_WIDTH = 3
N_HEADS = 8
HEAD_DIM = 64
D_ATTN = N_HEADS * HEAD_DIM
ROT_DIM = HEAD_DIM // 4
ROPE_THETA = 500000.0
IDX_HEADS = 4
IDX_DIM = 64
INDEX_TOPK = 256
EPS = 1e-6

LANES = 128
SUBLANES = 8
T_PROJ = 512
T_Q = 256
S_C = 256
T_OUT = 1024
IDX_W = 512
N_BISECT = 22
NEG = -1e30
RIDE = 16
V_ROWS = HEAD_DIM + 16
Q_SCALE = HEAD_DIM ** -0.5 * 1.4426950408889634
VMEM_LIMIT = 56 * 1024 * 1024

f32 = jnp.float32
bf16 = jnp.bfloat16


def _rope128(xb, c, sa, sb):
    return xb * c + pltpu.roll(xb, LANES - ROT_DIM // 2, 1) * sa + pltpu.roll(xb, ROT_DIM // 2, 1) * sb


def _proj_kernel(x_ref, g_ref, wm_ref, wi_ref, cw_ref, c_ref, sa_ref, sb_ref,
                 yc_ref, qtz_ref, k_ref, vta_ref, gz_ref, qit_ref, ki_ref, sgt_ref, ubuf):
    j = pl.program_id(1)
    T = T_PROJ

    @pl.when(j == 0)
    def _():
        ubuf[0:8, :] = jnp.zeros((8, D_CONV), f32)

    x = x_ref[0]
    xn = x * lax.rsqrt(jnp.mean(x * x, axis=-1, keepdims=True) + EPS) * g_ref[...]
    xn = xn.astype(bf16)

    wc_ref, wa_ref = 0, 4

    def proj(first, k):
        return jnp.dot(xn, wm_ref[:, (first + k) * 512:(first + k + 1) * 512], preferred_element_type=f32)

    u = proj(wc_ref, 2) * proj(wc_ref, 0)
    ubuf[8:8 + T, :] = u
    conv = (ubuf[6:6 + T, :] * cw_ref[0:1, :] + ubuf[7:7 + T, :] * cw_ref[1:2, :] + u * cw_ref[2:3, :])
    ubuf[0:8, :] = ubuf[T:T + 8, :]
    zc = proj(wc_ref, 3)
    yc_ref[0] = (proj(wc_ref, 1) * conv * (zc * jax.nn.sigmoid(zc))).astype(bf16)

    c = c_ref[...]
    sa = sa_ref[...]
    sb = sb_ref[...]

    qf = proj(wa_ref, 0)
    kf = proj(wa_ref, 1)
    vf = proj(wa_ref, 2)
    sub = lax.broadcasted_iota(jnp.int32, (LANES, S_C), 0)
    chunks = [slice(r * S_C, (r + 1) * S_C) for r in range(T // S_C)]
    for i in range(D_ATTN // LANES):
        sl = slice(i * LANES, (i + 1) * LANES)
        qr = _rope128(qf[:, sl], c, sa, sb) * Q_SCALE
        k_ref[0, i] = _rope128(kf[:, sl], c, sa, sb).astype(bf16)
        for r, rs in enumerate(chunks):
            qt = qr[rs].T
            qtz_ref[0, r, 2 * i] = jnp.where(sub < HEAD_DIM, qt, 0.0).astype(bf16)
            qtz_ref[0, r, 2 * i + 1] = jnp.where(sub >= HEAD_DIM, qt, 0.0).astype(bf16)
            vt = vf[rs, sl].T.astype(bf16)
            for h in (2 * i, 2 * i + 1):
                vta_ref[0, r, h, 0:HEAD_DIM, :] = vt[(h % 2) * HEAD_DIM:(h % 2 + 1) * HEAD_DIM, :]
                vta_ref[0, r, h, HEAD_DIM:V_ROWS, :] = jnp.ones((V_ROWS - HEAD_DIM, S_C), bf16)
    za = proj(wa_ref, 3)
    gz_ref[0] = za * jax.nn.sigmoid(za)

    pi = jnp.dot(xn, wi_ref[...], preferred_element_type=f32)
    wv = pi[:, 3 * LANES:4 * LANES]
    sgn = jnp.where(wv > 0, 1.0, jnp.where(wv < 0, -1.0, 0.0))
    for r, rs in enumerate(chunks):
        sgt_ref[0, r] = sgn[rs].T[0:SUBLANES, :]
    wabs = jnp.abs(wv) * ((IDX_DIM ** -0.5) * (IDX_HEADS ** -0.5))
    lane = lax.broadcasted_iota(jnp.int32, (T, LANES), 1)
    for i in range(2):
        sl = slice(i * LANES, (i + 1) * LANES)
        scale = jnp.where(lane < IDX_DIM, wabs[:, 2 * i:2 * i + 1], wabs[:, 2 * i + 1:2 * i + 2])
        qi = _rope128(pi[:, sl], c, sa, sb) * scale
        for r, rs in enumerate(chunks):
            qit_ref[0, r, sl, :] = qi[rs].T.astype(bf16)
    ki_ref[0] = _rope128(pi[:, 2 * LANES:3 * LANES], c, sa, sb)[:, 0:IDX_DIM].astype(bf16)


def _attn_kernel(qtz_ref, qit_ref, sgt_ref, gz_ref, k_ref, vta_ref, ki_ref, o_ref,
                 sc_ref, acc_ref, m_ref, l_ref, a0_ref, a1_ref, *, top_k, n_tiles):
    g = pl.program_id(1)
    TQ, SC = T_Q, S_C
    G = SC // SUBLANES
    nslot = sc_ref.shape[0] // 2
    nc = nslot - 1
    i = jnp.minimum(g, n_tiles - 1)
    ip = g - 1
    live = g < n_tiles
    cur = (g % 2) * nslot
    prev = nslot - cur

    def fold(a, op):
        return op(a.reshape(G, SUBLANES, TQ), axis=0)

    def rows(j):
        return pl.ds(pl.multiple_of(j * SC, SC), SC)

    def trips(n):
        return jnp.where(live, n, 0)

    t_loc = lax.broadcasted_iota(jnp.int32, (1, TQ), 1)
    lim_local = ((t_loc // CHUNK) + 1) * CHUNK
    limit = lim_local + i * TQ
    keff = jnp.minimum(limit, top_k).astype(f32)
    adm_diag = lax.broadcasted_iota(jnp.int32, (SC, TQ), 0) < lim_local

    sgt = sgt_ref[0, 0]
    sg = [sgt[h:h + 1, :] for h in range(IDX_HEADS)]

    def score_chunk(j):
        kc = ki_ref[0, rows(j), :]
        s = None
        for h in range(IDX_HEADS):
            l = jnp.dot(kc, qit_ref[0, 0, h * IDX_DIM:(h + 1) * IDX_DIM, :], preferred_element_type=f32)
            t = jnp.maximum(l, 0.0) * sg[h]
            s = t if s is None else s + t
        return s

    minus_inf = jnp.full((SC, TQ), -jnp.inf, f32)
    sc_ref[cur + nc] = minus_inf
    sc_ref[prev + nc] = minus_inf

    def p1_chunks(js, carry):
        mn, mx = carry
        for j in js:
            s = score_chunk(j)
            sc_ref[cur + j] = s
            mn, mx = jnp.minimum(mn, fold(s, jnp.min)), jnp.maximum(mx, fold(s, jnp.max))
        return mn, mx

    mn0 = jnp.full((SUBLANES, TQ), jnp.inf, f32)
    mx0 = jnp.full((SUBLANES, TQ), -jnp.inf, f32)
    U = 4
    carry = lax.fori_loop(0, trips(i // U), lambda jj, c: p1_chunks([U * jj + q for q in range(U)], c), (mn0, mx0))
    mn, mx = lax.fori_loop(trips(i // U * U), trips(i), lambda j, c: p1_chunks([j], c), carry)
    s = score_chunk(i)
    sc_ref[cur + i] = jnp.where(adm_diag, s, -jnp.inf)
    mn = jnp.minimum(mn, fold(jnp.where(adm_diag, s, jnp.inf), jnp.min))
    mx = jnp.maximum(mx, fold(jnp.where(adm_diag, s, -jnp.inf), jnp.max))
    lo = jnp.min(mn, axis=0, keepdims=True)
    hi = jnp.max(mx, axis=0, keepdims=True)

    acc_ref[...] = jnp.zeros(acc_ref.shape, f32)
    m_ref[...] = jnp.full(m_ref.shape, NEG, f32)
    l_ref[...] = jnp.zeros(l_ref.shape, f32)
    P = 2 * SUBLANES

    def all_sublanes(x8, op):
        for shift in (4, 2, 1):
            x8 = op(x8, pltpu.roll(x8, shift, 0))
        return x8

    def kv_chunk(c):
        return jnp.clip(c, 0, jnp.maximum(ip, 0))

    def logits_to(buf, c, between=lambda h: None):
        bias = sc_ref[jnp.where(c <= ip, prev + c, prev + nc)]
        for h in range(N_HEADS):
            kc = k_ref[0, h // 2, rows(kv_chunk(c)), :]
            buf[h] = (jnp.dot(kc, qtz_ref[0, 0, h], preferred_element_type=f32) + bias).astype(bf16)
            between(h)

    def softmax_pv(buf, c, between=lambda h: None):
        for h in range(N_HEADS):
            a = buf[h]
            cmax = jnp.max(a.reshape(SC // P, P, TQ), axis=0).astype(f32)
            m_old = m_ref[h]
            m_new = jnp.maximum(m_old, all_sublanes(jnp.maximum(cmax[0:SUBLANES], cmax[SUBLANES:P]), jnp.maximum))
            alpha = jnp.exp2(m_old - m_new)
            m_ref[h] = m_new
            mb = jnp.concatenate([m_new, m_new], axis=0).astype(bf16)
            p = jnp.exp2(a.reshape(SC // P, P, TQ) - mb[None]).reshape(SC, TQ)
            pv = jnp.dot(vta_ref[0, kv_chunk(c), h], p, preferred_element_type=f32)
            l_ref[h] = alpha * l_ref[h] + pv[HEAD_DIM:HEAD_DIM + SUBLANES]
            acc = acc_ref[h].reshape(HEAD_DIM // SUBLANES, SUBLANES, TQ) * alpha[None]
            acc_ref[h] = acc.reshape(HEAD_DIM, TQ) + pv[0:HEAD_DIM]
            between(h)

    def chunk_count(slot, mid8):
        return jnp.sum(jnp.where(sc_ref[slot].reshape(G, SUBLANES, TQ) >= mid8[None], 1.0, 0.0), axis=0)

    def update(state, mid, c, take=True):
        lo, hi, clo = state
        ge = c >= keff
        up = jnp.logical_and(ge, take)
        down = jnp.logical_and(jnp.logical_not(ge), take)
        return jnp.where(up, mid, lo), jnp.where(down, mid, hi), jnp.where(up, c, clo)

    n = i + 1
    n_trips = (g + 1) // 2
    per_step = (n + RIDE - 1) // RIDE
    n_ride = jnp.where(live, jnp.minimum(n_trips // per_step, N_BISECT), 0)
    logits_to(a0_ref, 0)

    def p4_body(jj, carry):
        lo, hi, clo, cnt, cursor, done = carry
        j = 2 * jj
        active = done < n_ride
        mid = 0.5 * lo + 0.5 * hi
        mid8 = jnp.broadcast_to(mid, (SUBLANES, TQ))

        counts = [cnt]

        def ride(stage):
            per_stage = RIDE // 4

            def between(h):
                for t in range(h * per_stage // N_HEADS, (h + 1) * per_stage // N_HEADS):
                    c = cursor + stage * per_stage + t
                    slot = jnp.where(jnp.logical_and(c <= i, active), cur + c, cur + nc)
                    counts[0] = counts[0] + chunk_count(slot, mid8)
            return between

        logits_to(a1_ref, j + 1, ride(0))
        softmax_pv(a0_ref, j, ride(1))
        logits_to(a0_ref, j + 2, ride(2))
        softmax_pv(a1_ref, j + 1, ride(3))
        cnt = counts[0]
        cursor = cursor + RIDE
        full = jnp.logical_and(cursor >= n, active)
        lo, hi, clo = update((lo, hi, clo), mid, jnp.sum(cnt, axis=0, keepdims=True), take=full)
        keep = jnp.where(full, 0.0, 1.0)
        return lo, hi, clo, cnt * keep, jnp.where(full, 0, cursor), done + full.astype(jnp.int32)

    clo = limit.astype(f32)
    lo, hi, clo, _, _, done = lax.fori_loop(
        0, n_trips, p4_body, (lo, hi, clo, jnp.zeros((SUBLANES, TQ), f32), jnp.int32(0), jnp.int32(0)))

    @pl.when(g >= 1)
    def _():
        for hp in range(N_HEADS // 2):
            pair = jnp.concatenate(
                [(acc_ref[h].reshape(HEAD_DIM // SUBLANES, SUBLANES, TQ) / l_ref[h][None]).reshape(HEAD_DIM, TQ)
                 for h in (2 * hp, 2 * hp + 1)], axis=0)
            sl = slice(hp * LANES, (hp + 1) * LANES)
            o_ref[0, :, sl] = (pair.T * gz_ref[0, :, sl]).astype(bf16)

    def count_ge(mid):
        mid8 = jnp.broadcast_to(mid, (SUBLANES, TQ))

        def some(js, cnt):
            for j in js:
                cnt = cnt + chunk_count(cur + j, mid8)
            return cnt
        cnt = lax.fori_loop(0, n // U, lambda jj, c: some([U * jj + q for q in range(U)], c),
                            jnp.zeros((SUBLANES, TQ), f32))
        cnt = lax.fori_loop(n // U * U, n, lambda j, c: some([j], c), cnt)
        return jnp.sum(cnt, axis=0, keepdims=True)

    def bisect(state):
        mid = 0.5 * state[0] + 0.5 * state[1]
        return update(state, mid, count_ge(mid))

    def snap(lo, hi):
        def some(js, carry):
            a, b = carry
            for j in js:
                s = sc_ref[cur + j]
                a = jnp.minimum(a, fold(jnp.where(s >= lo, s, jnp.inf), jnp.min))
                b = jnp.maximum(b, fold(jnp.where(s <= hi, s, -jnp.inf), jnp.max))
            return a, b
        carry = lax.fori_loop(0, trips(n // U), lambda jj, c: some([U * jj + q for q in range(U)], c), (mn0, mx0))
        a, b = lax.fori_loop(trips(n // U * U), trips(n), lambda j, c: some([j], c), carry)
        return jnp.min(a, axis=0, keepdims=True), jnp.max(b, axis=0, keepdims=True)

    lo, hi, clo = lax.fori_loop(0, trips(N_BISECT - done), lambda _, st: bisect(st), (lo, hi, clo))
    vlo, vhi = snap(lo, hi)

    def unconverged(lo, hi, clo, vlo, vhi):
        bad = jnp.logical_and(clo != keff, vlo != vhi)
        return jnp.logical_and(live, jnp.max(jnp.where(bad, 1.0, 0.0)) > 0.0)

    def refine(state):
        lo, hi, clo, _, _ = state
        lo, hi, clo = bisect((lo, hi, clo))
        vlo, vhi = snap(lo, hi)
        return lo, hi, clo, vlo, vhi

    lo, hi, clo, vlo, vhi = lax.while_loop(lambda st: unconverged(*st), refine, (lo, hi, clo, vlo, vhi))
    tau = vlo
    excess = clo - keff

    ka = lax.broadcasted_iota(jnp.int32, (SC, SC), 0)
    kb = lax.broadcasted_iota(jnp.int32, (SC, SC), 1)
    upper = jnp.where(kb > ka, 1.0, 0.0).astype(bf16)

    def p3_chunks(slots, carry):
        loaded = [(slot, sc_ref[slot]) for slot in slots]
        for slot, s in loaded:
            eq = s == tau
            eqf = jnp.where(eq, 1.0, 0.0)
            later = jnp.dot(upper, eqf.astype(bf16), preferred_element_type=f32) + carry
            sel = jnp.logical_or(s > tau, jnp.logical_and(eq, later >= excess))
            sc_ref[slot] = jnp.where(sel, 0.0, NEG)
            carry = later[0:1, :] + eqf[0:1, :]
        return carry

    rest = n % U
    carry = lax.fori_loop(0, trips(rest), lambda r, c: p3_chunks([cur + i - r], c), jnp.zeros((1, TQ), f32))
    half = (n // U) % 2
    carry = lax.fori_loop(0, trips(half),
                          lambda _, c: p3_chunks([cur + i - rest - q for q in range(U)], c), carry)
    top = i - rest - U * half
    lax.fori_loop(0, trips(n // (2 * U)),
                  lambda jj, c: p3_chunks([cur + top - 2 * U * jj - q for q in range(2 * U)], c), carry)


def _out_kernel(x_ref, yc_ref, ya_ref, woc_ref, woa_ref, g_ref, o_ref):
    y = (jnp.dot(yc_ref[0], woc_ref[...], preferred_element_type=f32)
         + jnp.dot(ya_ref[0], woa_ref[...], preferred_element_type=f32))
    yn = y * lax.rsqrt(jnp.mean(y * y, axis=-1, keepdims=True) + EPS) * g_ref[...]
    o_ref[0] = x_ref[0] + yn


def _rope_tables(S):
    half = ROT_DIM // 2
    inv_freq = jnp.power(jnp.float32(ROPE_THETA), -jnp.arange(half, dtype=f32) / half)
    ang = jnp.arange(S, dtype=jnp.int32).astype(f32)[:, None] * inv_freq[None, :]
    cos, sin = jnp.cos(ang), jnp.sin(ang)
    ones = jnp.ones((S, HEAD_DIM - ROT_DIM), f32)
    zeros = jnp.zeros((S, HEAD_DIM - ROT_DIM), f32)
    zh = jnp.zeros((S, half), f32)
    c = jnp.concatenate([cos, cos, ones], axis=1)
    sa = jnp.concatenate([-sin, zh, zeros], axis=1)
    sb = jnp.concatenate([zh, sin, zeros], axis=1)
    rep = LANES // HEAD_DIM
    return jnp.tile(c, (1, rep)), jnp.tile(sa, (1, rep)), jnp.tile(sb, (1, rep))


def _const_spec(shape):
    nd = len(shape)
    return pl.BlockSpec(shape, lambda b, j: (0,) * nd, pipeline_mode=pl.Buffered(1))


def _layer(x, g_pre, w_in, conv_w, w_out, g_post):
    B, S, _ = x.shape
    assert S % T_PROJ == 0 and S % T_Q == 0 and S % T_OUT == 0 and T_Q == S_C and T_PROJ % S_C == 0
    top_k = min(INDEX_TOPK, S // 4)
    nc = S // S_C

    o = 4 * D_CONV + 4 * D_ATTN
    wm = w_in[:, 0:o]
    nq = IDX_HEADS * IDX_DIM
    wi = jnp.zeros((D_MODEL, IDX_W), bf16)
    wi = wi.at[:, 0:nq + IDX_DIM].set(w_in[:, o:o + nq + IDX_DIM])
    wi = wi.at[:, 3 * LANES:3 * LANES + IDX_HEADS].set(w_in[:, o + nq + IDX_DIM:o + nq + IDX_DIM + IDX_HEADS])
    woc = w_out[0:D_CONV]
    woa = w_out[D_CONV:]
    c_t, sa_t, sb_t = _rope_tables(S)

    cp = functools.partial(pltpu.CompilerParams, vmem_limit_bytes=VMEM_LIMIT)

    tile = lambda w: pl.BlockSpec((1, T_PROJ, w), lambda b, j: (b, j, 0))
    ttile = lambda *shape: pl.BlockSpec((1, T_PROJ // S_C) + shape, lambda b, j: (b, j) + (0,) * len(shape))
    tab = pl.BlockSpec((T_PROJ, LANES), lambda b, j: (j, 0))
    sds = jax.ShapeDtypeStruct
    yc, qtz, k, vta, gz, qit, ki, sgt = pl.pallas_call(
        _proj_kernel,
        grid=(B, S // T_PROJ),
        in_specs=[tile(D_MODEL), _const_spec((1, D_MODEL)), _const_spec((D_MODEL, 4 * D_CONV + 4 * D_ATTN)),
                  _const_spec((D_MODEL, IDX_W)),
                  _const_spec((CONV_WIDTH, D_CONV)), tab, tab, tab],
        out_specs=[tile(D_CONV), ttile(N_HEADS, LANES, S_C),
                   pl.BlockSpec((1, D_ATTN // LANES, T_PROJ, LANES), lambda b, j: (b, 0, j, 0)),
                   ttile(N_HEADS, V_ROWS, S_C),
                   tile(D_ATTN), ttile(nq, S_C), tile(IDX_DIM), ttile(SUBLANES, S_C)],
        out_shape=[sds((B, S, D_CONV), bf16), sds((B, nc, N_HEADS, LANES, S_C), bf16),
                   sds((B, D_ATTN // LANES, S, LANES), bf16), sds((B, nc, N_HEADS, V_ROWS, S_C), bf16),
                   sds((B, S, D_ATTN), f32), sds((B, nc, nq, S_C), bf16),
                   sds((B, S, IDX_DIM), bf16), sds((B, nc, SUBLANES, S_C), f32)],
        scratch_shapes=[pltpu.VMEM((T_PROJ + 8, D_CONV), f32)],
        compiler_params=cp(dimension_semantics=("arbitrary", "arbitrary")),
        name="proj",
    )(x, g_pre.reshape(1, D_MODEL), wm, wi, conv_w, c_t, sa_t, sb_t)

    nq_tiles = S // T_Q
    lag = lambda g: jnp.maximum(g - 1, 0)
    sel = lambda g: jnp.minimum(g, nq_tiles - 1)
    whole = lambda *shape: pl.BlockSpec((1,) + shape, lambda b, g: (b,) + (0,) * len(shape),
                                        pipeline_mode=pl.Buffered(1))
    ya = pl.pallas_call(
        functools.partial(_attn_kernel, top_k=top_k, n_tiles=nq_tiles),
        grid=(B, nq_tiles + 1),
        in_specs=[pl.BlockSpec((1, 1, N_HEADS, LANES, T_Q), lambda b, g: (b, lag(g), 0, 0, 0)),
                  pl.BlockSpec((1, 1, nq, T_Q), lambda b, g: (b, sel(g), 0, 0)),
                  pl.BlockSpec((1, 1, SUBLANES, T_Q), lambda b, g: (b, sel(g), 0, 0)),
                  pl.BlockSpec((1, T_Q, D_ATTN), lambda b, g: (b, lag(g), 0)),
                  whole(D_ATTN // LANES, S, LANES), whole(nc, N_HEADS, V_ROWS, S_C), whole(S, IDX_DIM)],
        out_specs=pl.BlockSpec((1, T_Q, D_ATTN), lambda b, g: (b, lag(g), 0)),
        out_shape=sds((B, S, D_ATTN), bf16),
        scratch_shapes=[pltpu.VMEM((2 * (nc + 1), S_C, T_Q), f32), pltpu.VMEM((N_HEADS, HEAD_DIM, T_Q), f32),
                        pltpu.VMEM((N_HEADS, SUBLANES, T_Q), f32), pltpu.VMEM((N_HEADS, SUBLANES, T_Q), f32),
                        pltpu.VMEM((N_HEADS, S_C, T_Q), bf16), pltpu.VMEM((N_HEADS, S_C, T_Q), bf16)],
        compiler_params=cp(dimension_semantics=("arbitrary", "arbitrary")),
        name="attn",
    )(qtz, qit, sgt, gz, k, vta, ki)

    otile = lambda w: pl.BlockSpec((1, T_OUT, w), lambda b, j: (b, j, 0))
    return pl.pallas_call(
        _out_kernel,
        grid=(B, S // T_OUT),
        in_specs=[otile(D_MODEL), otile(D_CONV), otile(D_ATTN), _const_spec((D_CONV, D_MODEL)),
                  _const_spec((D_ATTN, D_MODEL)), _const_spec((1, D_MODEL))],
        out_specs=otile(D_MODEL),
        out_shape=sds((B, S, D_MODEL), f32),
        compiler_params=cp(dimension_semantics=("arbitrary", "arbitrary")),
        name="outproj",
    )(x, yc, ya, woc, woa, g_post.reshape(1, D_MODEL))


def kernel(x, norm_pre_g, w_in, conv_w, w_out, norm_post_g):
    w_in, w_out = w_in.astype(bf16), w_out.astype(bf16)
    for l in range(norm_pre_g.shape[0]):
        x = _layer(x, norm_pre_g[l], w_in[l], conv_w[l], w_out[l], norm_post_g[l])
    return x
```

```python
import functools

import jax
import jax.numpy as jnp
from jax import lax
from jax.experimental import pallas as pl
from jax.experimental.pallas import tpu as pltpu

D_MODEL = 1024
CHUNK = 64
D_CONV = 512
CONV_WIDTH = 3
N_HEADS = 8
HEAD_DIM = 64
D_ATTN = N_HEADS * HEAD_DIM
ROT_DIM = HEAD_DIM // 4
ROPE_THETA = 500000.0
IDX_HEADS = 4
IDX_DIM = 64
INDEX_TOPK = 256
EPS = 1e-6

LANES = 128
SUBLANES = 8
T_PROJ = 1024
T_Q = 256
S_C = 256
T_OUT = 1024
IDX_W = 512
N_BISECT = 22
NEG = -1e30
RIDE = 16
V_ROWS = HEAD_DIM + 16
Q_SCALE = HEAD_DIM ** -0.5 * 1.4426950408889634
VMEM_LIMIT = 56 * 1024 * 1024

f32 = jnp.float32
bf16 = jnp.bfloat16


def _rope128(xb, c, sa, sb):
    return xb * c + pltpu.roll(xb, LANES - ROT_DIM // 2, 1) * sa + pltpu.roll(xb, ROT_DIM // 2, 1) * sb


def _proj_kernel(x_ref, g_ref, wm_ref, wi_ref, cw_ref, c_ref, sa_ref, sb_ref,
                 yc_ref, qtz_ref, k_ref, vta_ref, gz_ref, qit_ref, ki_ref, sgt_ref, ubuf):
    j = pl.program_id(1)
    T = T_PROJ

    @pl.when(j == 0)
    def _():
        ubuf[0:8, :] = jnp.zeros((8, D_CONV), f32)

    x = x_ref[0]
    xn = x * lax.rsqrt(jnp.mean(x * x, axis=-1, keepdims=True) + EPS) * g_ref[...]
    xn = xn.astype(bf16)

    wc_ref, wa_ref = 0, 4

    def proj(first, k):
        return jnp.dot(xn, wm_ref[:, (first + k) * 512:(first + k + 1) * 512], preferred_element_type=f32)

    u = proj(wc_ref, 2) * proj(wc_ref, 0)
    ubuf[8:8 + T, :] = u
    conv = (ubuf[6:6 + T, :] * cw_ref[0:1, :] + ubuf[7:7 + T, :] * cw_ref[1:2, :] + u * cw_ref[2:3, :])
    ubuf[0:8, :] = ubuf[T:T + 8, :]
    zc = proj(wc_ref, 3)
    yc_ref[0] = (proj(wc_ref, 1) * conv * (zc * jax.nn.sigmoid(zc))).astype(bf16)

    c = c_ref[...]
    sa = sa_ref[...]
    sb = sb_ref[...]

    qf = proj(wa_ref, 0)
    kf = proj(wa_ref, 1)
    vf = proj(wa_ref, 2)
    sub = lax.broadcasted_iota(jnp.int32, (LANES, S_C), 0)
    chunks = [slice(r * S_C, (r + 1) * S_C) for r in range(T // S_C)]
    for i in range(D_ATTN // LANES):
        sl = slice(i * LANES, (i + 1) * LANES)
        qr = _rope128(qf[:, sl], c, sa, sb) * Q_SCALE
        k_ref[0, i] = _rope128(kf[:, sl], c, sa, sb).astype(bf16)
        for r, rs in enumerate(chunks):
            qt = qr[rs].T
            qtz_ref[0, r, 2 * i] = jnp.where(sub < HEAD_DIM, qt, 0.0).astype(bf16)
            qtz_ref[0, r, 2 * i + 1] = jnp.where(sub >= HEAD_DIM, qt, 0.0).astype(bf16)
            vt = vf[rs, sl].T.astype(bf16)
            for h in (2 * i, 2 * i + 1):
                vta_ref[0, r, h, 0:HEAD_DIM, :] = vt[(h % 2) * HEAD_DIM:(h % 2 + 1) * HEAD_DIM, :]
                vta_ref[0, r, h, HEAD_DIM:V_ROWS, :] = jnp.ones((V_ROWS - HEAD_DIM, S_C), bf16)
    za = proj(wa_ref, 3)
    gz_ref[0] = za * jax.nn.sigmoid(za)

    pi = jnp.dot(xn, wi_ref[...], preferred_element_type=f32)
    wv = pi[:, 3 * LANES:4 * LANES]
    sgn = jnp.where(wv > 0, 1.0, jnp.where(wv < 0, -1.0, 0.0))
    for r, rs in enumerate(chunks):
        sgt_ref[0, r] = sgn[rs].T[0:SUBLANES, :]
    wabs = jnp.abs(wv) * ((IDX_DIM ** -0.5) * (IDX_HEADS ** -0.5))
    lane = lax.broadcasted_iota(jnp.int32, (T, LANES), 1)
    for i in range(2):
        sl = slice(i * LANES, (i + 1) * LANES)
        scale = jnp.where(lane < IDX_DIM, wabs[:, 2 * i:2 * i + 1], wabs[:, 2 * i + 1:2 * i + 2])
        qi = _rope128(pi[:, sl], c, sa, sb) * scale
        for r, rs in enumerate(chunks):
            qit_ref[0, r, sl, :] = qi[rs].T.astype(bf16)
    ki_ref[0] = _rope128(pi[:, 2 * LANES:3 * LANES], c, sa, sb)[:, 0:IDX_DIM].astype(bf16)


def _attn_kernel(qtz_ref, qit_ref, sgt_ref, gz_ref, k_ref, vta_ref, ki_ref, o_ref,
                 sc_ref, acc_ref, m_ref, l_ref, a0_ref, a1_ref, *, top_k, n_tiles):
    g = pl.program_id(1)
    TQ, SC = T_Q, S_C
    G = SC // SUBLANES
    nslot = sc_ref.shape[0] // 2
    nc = nslot - 1
    i = jnp.minimum(g, n_tiles - 1)
    ip = g - 1
    live = g < n_tiles
    cur = (g % 2) * nslot
    prev = nslot - cur

    def fold(a, op):
        return op(a.reshape(G, SUBLANES, TQ), axis=0)

    def rows(j):
        return pl.ds(pl.multiple_of(j * SC, SC), SC)

    def trips(n):
        return jnp.where(live, n, 0)

    t_loc = lax.broadcasted_iota(jnp.int32, (1, TQ), 1)
    lim_local = ((t_loc // CHUNK) + 1) * CHUNK
    limit = lim_local + i * TQ
    keff = jnp.minimum(limit, top_k).astype(f32)
    adm_diag = lax.broadcasted_iota(jnp.int32, (SC, TQ), 0) < lim_local

    sgt = sgt_ref[0, 0]
    sg = [sgt[h:h + 1, :] for h in range(IDX_HEADS)]

    def score_chunk(j):
        kc = ki_ref[0, rows(j), :]
        s = None
        for h in range(IDX_HEADS):
            l = jnp.dot(kc, qit_ref[0, 0, h * IDX_DIM:(h + 1) * IDX_DIM, :], preferred_element_type=f32)
            t = jnp.maximum(l, 0.0) * sg[h]
            s = t if s is None else s + t
        return s

    minus_inf = jnp.full((SC, TQ), -jnp.inf, f32)
    sc_ref[cur + nc] = minus_inf
    sc_ref[prev + nc] = minus_inf

    def p1_chunks(js, carry):
        mn, mx = carry
        for j in js:
            s = score_chunk(j)
            sc_ref[cur + j] = s
            mn, mx = jnp.minimum(mn, fold(s, jnp.min)), jnp.maximum(mx, fold(s, jnp.max))
        return mn, mx

    mn0 = jnp.full((SUBLANES, TQ), jnp.inf, f32)
    mx0 = jnp.full((SUBLANES, TQ), -jnp.inf, f32)
    U = 4
    carry = lax.fori_loop(0, trips(i // U), lambda jj, c: p1_chunks([U * jj + q for q in range(U)], c), (mn0, mx0))
    mn, mx = lax.fori_loop(trips(i // U * U), trips(i), lambda j, c: p1_chunks([j], c), carry)
    s = score_chunk(i)
    sc_ref[cur + i] = jnp.where(adm_diag, s, -jnp.inf)
    mn = jnp.minimum(mn, fold(jnp.where(adm_diag, s, jnp.inf), jnp.min))
    mx = jnp.maximum(mx, fold(jnp.where(adm_diag, s, -jnp.inf), jnp.max))
    lo = jnp.min(mn, axis=0, keepdims=True)
    hi = jnp.max(mx, axis=0, keepdims=True)

    acc_ref[...] = jnp.zeros(acc_ref.shape, f32)
    m_ref[...] = jnp.full(m_ref.shape, NEG, f32)
    l_ref[...] = jnp.zeros(l_ref.shape, f32)
    P = 2 * SUBLANES

    def all_sublanes(x8, op):
        for shift in (4, 2, 1):
            x8 = op(x8, pltpu.roll(x8, shift, 0))
        return x8

    def kv_chunk(c):
        return jnp.clip(c, 0, jnp.maximum(ip, 0))

    def logits_to(buf, c, between=lambda h: None):
        bias = sc_ref[jnp.where(c <= ip, prev + c, prev + nc)]
        for h in range(N_HEADS):
            kc = k_ref[0, h // 2, rows(kv_chunk(c)), :]
            buf[h] = (jnp.dot(kc, qtz_ref[0, 0, h], preferred_element_type=f32) + bias).astype(bf16)
            between(h)

    def softmax_pv(buf, c, between=lambda h: None):
        for h in range(N_HEADS):
            a = buf[h]
            cmax = jnp.max(a.reshape(SC // P, P, TQ), axis=0).astype(f32)
            m_old = m_ref[h]
            m_new = jnp.maximum(m_old, all_sublanes(jnp.maximum(cmax[0:SUBLANES], cmax[SUBLANES:P]), jnp.maximum))
            alpha = jnp.exp2(m_old - m_new)
            m_ref[h] = m_new
            mb = jnp.concatenate([m_new, m_new], axis=0).astype(bf16)
            p = jnp.exp2(a.reshape(SC // P, P, TQ) - mb[None]).reshape(SC, TQ)
            pv = jnp.dot(vta_ref[0, kv_chunk(c), h], p, preferred_element_type=f32)
            l_ref[h] = alpha * l_ref[h] + pv[HEAD_DIM:HEAD_DIM + SUBLANES]
            acc = acc_ref[h].reshape(HEAD_DIM // SUBLANES, SUBLANES, TQ) * alpha[None]
            acc_ref[h] = acc.reshape(HEAD_DIM, TQ) + pv[0:HEAD_DIM]
            between(h)

    def chunk_count(slot, mid8):
        return jnp.sum(jnp.where(sc_ref[slot].reshape(G, SUBLANES, TQ) >= mid8[None], 1.0, 0.0), axis=0)

    def update(state, mid, c, take=True):
        lo, hi, clo = state
        ge = c >= keff
        up = jnp.logical_and(ge, take)
        down = jnp.logical_and(jnp.logical_not(ge), take)
        return jnp.where(up, mid, lo), jnp.where(down, mid, hi), jnp.where(up, c, clo)

    n = i + 1
    n_trips = (g + 1) // 2
    per_step = (n + RIDE - 1) // RIDE
    n_ride = jnp.where(live, jnp.minimum(n_trips // per_step, N_BISECT), 0)
    logits_to(a0_ref, 0)

    def p4_body(jj, carry):
        lo, hi, clo, cnt, cursor, done = carry
        j = 2 * jj
        active = done < n_ride
        mid = 0.5 * lo + 0.5 * hi
        mid8 = jnp.broadcast_to(mid, (SUBLANES, TQ))

        counts = [cnt]

        def ride(stage):
            per_stage = RIDE // 4

            def between(h):
                for t in range(h * per_stage // N_HEADS, (h + 1) * per_stage // N_HEADS):
                    c = cursor + stage * per_stage + t
                    slot = jnp.where(jnp.logical_and(c <= i, active), cur + c, cur + nc)
                    counts[0] = counts[0] + chunk_count(slot, mid8)
            return between

        logits_to(a1_ref, j + 1, ride(0))
        softmax_pv(a0_ref, j, ride(1))
        logits_to(a0_ref, j + 2, ride(2))
        softmax_pv(a1_ref, j + 1, ride(3))
        cnt = counts[0]
        cursor = cursor + RIDE
        full = jnp.logical_and(cursor >= n, active)
        lo, hi, clo = update((lo, hi, clo), mid, jnp.sum(cnt, axis=0, keepdims=True), take=full)
        keep = jnp.where(full, 0.0, 1.0)
        return lo, hi, clo, cnt * keep, jnp.where(full, 0, cursor), done + full.astype(jnp.int32)

    clo = limit.astype(f32)
    lo, hi, clo, _, _, done = lax.fori_loop(
        0, n_trips, p4_body, (lo, hi, clo, jnp.zeros((SUBLANES, TQ), f32), jnp.int32(0), jnp.int32(0)))

    @pl.when(g >= 1)
    def _():
        for hp in range(N_HEADS // 2):
            pair = jnp.concatenate(
                [(acc_ref[h].reshape(HEAD_DIM // SUBLANES, SUBLANES, TQ) / l_ref[h][None]).reshape(HEAD_DIM, TQ)
                 for h in (2 * hp, 2 * hp + 1)], axis=0)
            sl = slice(hp * LANES, (hp + 1) * LANES)
            o_ref[0, :, sl] = (pair.T * gz_ref[0, :, sl]).astype(bf16)

    def count_ge(mid):
        mid8 = jnp.broadcast_to(mid, (SUBLANES, TQ))

        def some(js, cnt):
            for j in js:
                cnt = cnt + chunk_count(cur + j, mid8)
            return cnt
        cnt = lax.fori_loop(0, n // U, lambda jj, c: some([U * jj + q for q in range(U)], c),
                            jnp.zeros((SUBLANES, TQ), f32))
        cnt = lax.fori_loop(n // U * U, n, lambda j, c: some([j], c), cnt)
        return jnp.sum(cnt, axis=0, keepdims=True)

    def bisect(state):
        mid = 0.5 * state[0] + 0.5 * state[1]
        return update(state, mid, count_ge(mid))

    def snap(lo, hi):
        def some(js, carry):
            a, b = carry
            for j in js:
                s = sc_ref[cur + j]
                a = jnp.minimum(a, fold(jnp.where(s >= lo, s, jnp.inf), jnp.min))
                b = jnp.maximum(b, fold(jnp.where(s <= hi, s, -jnp.inf), jnp.max))
            return a, b
        carry = lax.fori_loop(0, trips(n // U), lambda jj, c: some([U * jj + q for q in range(U)], c), (mn0, mx0))
        a, b = lax.fori_loop(trips(n // U * U), trips(n), lambda j, c: some([j], c), carry)
        return jnp.min(a, axis=0, keepdims=True), jnp.max(b, axis=0, keepdims=True)

    lo, hi, clo = lax.fori_loop(0, trips(N_BISECT - done), lambda _, st: bisect(st), (lo, hi, clo))
    vlo, vhi = snap(lo, hi)

    def unconverged(lo, hi, clo, vlo, vhi):
        bad = jnp.logical_and(clo != keff, vlo != vhi)
        return jnp.logical_and(live, jnp.max(jnp.where(bad, 1.0, 0.0)) > 0.0)

    def refine(state):
        lo, hi, clo, _, _ = state
        lo, hi, clo = bisect((lo, hi, clo))
        vlo, vhi = snap(lo, hi)
        return lo, hi, clo, vlo, vhi

    lo, hi, clo, vlo, vhi = lax.while_loop(lambda st: unconverged(*st), refine, (lo, hi, clo, vlo, vhi))
    tau = vlo
    excess = clo - keff

    ka = lax.broadcasted_iota(jnp.int32, (SC, SC), 0)
    kb = lax.broadcasted_iota(jnp.int32, (SC, SC), 1)
    upper = jnp.where(kb > ka, 1.0, 0.0).astype(bf16)

    def p3_chunks(slots, carry):
        loaded = [(slot, sc_ref[slot]) for slot in slots]
        for slot, s in loaded:
            eq = s == tau
            eqf = jnp.where(eq, 1.0, 0.0)
            later = jnp.dot(upper, eqf.astype(bf16), preferred_element_type=f32) + carry
            sel = jnp.logical_or(s > tau, jnp.logical_and(eq, later >= excess))
            sc_ref[slot] = jnp.where(sel, 0.0, NEG)
            carry = later[0:1, :] + eqf[0:1, :]
        return carry

    rest = n % U
    carry = lax.fori_loop(0, trips(rest), lambda r, c: p3_chunks([cur + i - r], c), jnp.zeros((1, TQ), f32))
    half = (n // U) % 2
    carry = lax.fori_loop(0, trips(half),
                          lambda _, c: p3_chunks([cur + i - rest - q for q in range(U)], c), carry)
    top = i - rest - U * half
    lax.fori_loop(0, trips(n // (2 * U)),
                  lambda jj, c: p3_chunks([cur + top - 2 * U * jj - q for q in range(2 * U)], c), carry)


def _out_kernel(x_ref, yc_ref, ya_ref, woc_ref, woa_ref, g_ref, o_ref):
    y = (jnp.dot(yc_ref[0], woc_ref[...], preferred_element_type=f32)
         + jnp.dot(ya_ref[0], woa_ref[...], preferred_element_type=f32))
    yn = y * lax.rsqrt(jnp.mean(y * y, axis=-1, keepdims=True) + EPS) * g_ref[...]
    o_ref[0] = x_ref[0] + yn


def _rope_tables(S):
    half = ROT_DIM // 2
    inv_freq = jnp.power(jnp.float32(ROPE_THETA), -jnp.arange(half, dtype=f32) / half)
    ang = jnp.arange(S, dtype=jnp.int32).astype(f32)[:, None] * inv_freq[None, :]
    cos, sin = jnp.cos(ang), jnp.sin(ang)
    ones = jnp.ones((S, HEAD_DIM - ROT_DIM), f32)
    zeros = jnp.zeros((S, HEAD_DIM - ROT_DIM), f32)
    zh = jnp.zeros((S, half), f32)
    c = jnp.concatenate([cos, cos, ones], axis=1)
    sa = jnp.concatenate([-sin, zh, zeros], axis=1)
    sb = jnp.concatenate([zh, sin, zeros], axis=1)
    rep = LANES // HEAD_DIM
    return jnp.tile(c, (1, rep)), jnp.tile(sa, (1, rep)), jnp.tile(sb, (1, rep))


def _const_spec(shape):
    nd = len(shape)
    return pl.BlockSpec(shape, lambda b, j: (0,) * nd, pipeline_mode=pl.Buffered(1))


def _layer(x, g_pre, w_in, conv_w, w_out, g_post):
    B, S, _ = x.shape
    assert S % T_PROJ == 0 and S % T_Q == 0 and S % T_OUT == 0 and T_Q == S_C and T_PROJ % S_C == 0
    top_k = min(INDEX_TOPK, S // 4)
    nc = S // S_C

    o = 4 * D_CONV + 4 * D_ATTN
    wm = w_in[:, 0:o]
    nq = IDX_HEADS * IDX_DIM
    wi = jnp.zeros((D_MODEL, IDX_W), bf16)
    wi = wi.at[:, 0:nq + IDX_DIM].set(w_in[:, o:o + nq + IDX_DIM])
    wi = wi.at[:, 3 * LANES:3 * LANES + IDX_HEADS].set(w_in[:, o + nq + IDX_DIM:o + nq + IDX_DIM + IDX_HEADS])
    woc = w_out[0:D_CONV]
    woa = w_out[D_CONV:]
    c_t, sa_t, sb_t = _rope_tables(S)

    cp = functools.partial(pltpu.CompilerParams, vmem_limit_bytes=VMEM_LIMIT)

    tile = lambda w: pl.BlockSpec((1, T_PROJ, w), lambda b, j: (b, j, 0))
    ttile = lambda *shape: pl.BlockSpec((1, T_PROJ // S_C) + shape, lambda b, j: (b, j) + (0,) * len(shape))
    tab = pl.BlockSpec((T_PROJ, LANES), lambda b, j: (j, 0))
    sds = jax.ShapeDtypeStruct
    yc, qtz, k, vta, gz, qit, ki, sgt = pl.pallas_call(
        _proj_kernel,
        grid=(B, S // T_PROJ),
        in_specs=[tile(D_MODEL), _const_spec((1, D_MODEL)), _const_spec((D_MODEL, 4 * D_CONV + 4 * D_ATTN)),
                  _const_spec((D_MODEL, IDX_W)),
                  _const_spec((CONV_WIDTH, D_CONV)), tab, tab, tab],
        out_specs=[tile(D_CONV), ttile(N_HEADS, LANES, S_C),
                   pl.BlockSpec((1, D_ATTN // LANES, T_PROJ, LANES), lambda b, j: (b, 0, j, 0)),
                   ttile(N_HEADS, V_ROWS, S_C),
                   tile(D_ATTN), ttile(nq, S_C), tile(IDX_DIM), ttile(SUBLANES, S_C)],
        out_shape=[sds((B, S, D_CONV), bf16), sds((B, nc, N_HEADS, LANES, S_C), bf16),
                   sds((B, D_ATTN // LANES, S, LANES), bf16), sds((B, nc, N_HEADS, V_ROWS, S_C), bf16),
                   sds((B, S, D_ATTN), f32), sds((B, nc, nq, S_C), bf16),
                   sds((B, S, IDX_DIM), bf16), sds((B, nc, SUBLANES, S_C), f32)],
        scratch_shapes=[pltpu.VMEM((T_PROJ + 8, D_CONV), f32)],
        compiler_params=cp(dimension_semantics=("arbitrary", "arbitrary")),
        name="proj",
    )(x, g_pre.reshape(1, D_MODEL), wm, wi, conv_w, c_t, sa_t, sb_t)

    nq_tiles = S // T_Q
    lag = lambda g: jnp.maximum(g - 1, 0)
    sel = lambda g: jnp.minimum(g, nq_tiles - 1)
    whole = lambda *shape: pl.BlockSpec((1,) + shape, lambda b, g: (b,) + (0,) * len(shape),
                                        pipeline_mode=pl.Buffered(1))
    ya = pl.pallas_call(
        functools.partial(_attn_kernel, top_k=top_k, n_tiles=nq_tiles),
        grid=(B, nq_tiles + 1),
        in_specs=[pl.BlockSpec((1, 1, N_HEADS, LANES, T_Q), lambda b, g: (b, lag(g), 0, 0, 0)),
                  pl.BlockSpec((1, 1, nq, T_Q), lambda b, g: (b, sel(g), 0, 0)),
                  pl.BlockSpec((1, 1, SUBLANES, T_Q), lambda b, g: (b, sel(g), 0, 0)),
                  pl.BlockSpec((1, T_Q, D_ATTN), lambda b, g: (b, lag(g), 0)),
                  whole(D_ATTN // LANES, S, LANES), whole(nc, N_HEADS, V_ROWS, S_C), whole(S, IDX_DIM)],
        out_specs=pl.BlockSpec((1, T_Q, D_ATTN), lambda b, g: (b, lag(g), 0)),
        out_shape=sds((B, S, D_ATTN), bf16),
        scratch_shapes=[pltpu.VMEM((2 * (nc + 1), S_C, T_Q), f32), pltpu.VMEM((N_HEADS, HEAD_DIM, T_Q), f32),
                        pltpu.VMEM((N_HEADS, SUBLANES, T_Q), f32), pltpu.VMEM((N_HEADS, SUBLANES, T_Q), f32),
                        pltpu.VMEM((N_HEADS, S_C, T_Q), bf16), pltpu.VMEM((N_HEADS, S_C, T_Q), bf16)],
        compiler_params=cp(dimension_semantics=("arbitrary", "arbitrary")),
        name="attn",
    )(qtz, qit, sgt, gz, k, vta, ki)

    otile = lambda w: pl.BlockSpec((1, T_OUT, w), lambda b, j: (b, j, 0))
    return pl.pallas_call(
        _out_kernel,
        grid=(B, S // T_OUT),
        in_specs=[otile(D_MODEL), otile(D_CONV), otile(D_ATTN), _const_spec((D_CONV, D_MODEL)),
                  _const_spec((D_ATTN, D_MODEL)), _const_spec((1, D_MODEL))],
        out_specs=otile(D_MODEL),
        out_shape=sds((B, S, D_MODEL), f32),
        compiler_params=cp(dimension_semantics=("arbitrary", "arbitrary")),
        name="outproj",
    )(x, yc, ya, woc, woa, g_post.reshape(1, D_MODEL))


def kernel(x, norm_pre_g, w_in, conv_w, w_out, norm_post_g):
    w_in, w_out = w_in.astype(bf16), w_out.astype(bf16)
    for l in range(norm_pre_g.shape[0]):
        x = _layer(x, norm_pre_g[l], w_in[l], conv_w[l], w_out[l], norm_post_g[l])
    return x
```

```python
import functools

import jax
import jax.numpy as jnp
from jax import lax
from jax.experimental import pallas as pl
from jax.experimental.pallas import tpu as pltpu

D_MODEL = 1024
CHUNK = 64
D_CONV = 512
CONV_WIDTH = 3
N_HEADS = 8
HEAD_DIM = 64
D_ATTN = N_HEADS * HEAD_DIM
ROT_DIM = HEAD_DIM // 4
ROPE_THETA = 500000.0
IDX_HEADS = 4
IDX_DIM = 64
INDEX_TOPK = 256
EPS = 1e-6

LANES = 128
SUBLANES = 8
T_PROJ = 1024
T_Q = 256
S_C = 256
T_OUT = 1024
IDX_W = 512
N_BISECT = 22
NEG = -1e30
RIDE = 16
V_ROWS = HEAD_DIM + 16
Q_SCALE = HEAD_DIM ** -0.5 * 1.4426950408889634
VMEM_LIMIT = 56 * 1024 * 1024

f32 = jnp.float32
bf16 = jnp.bfloat16


def _rope128(xb, c, sa, sb):
    return xb * c + pltpu.roll(xb, LANES - ROT_DIM // 2, 1) * sa + pltpu.roll(xb, ROT_DIM // 2, 1) * sb


def _proj_kernel(x_ref, g_ref, wm_ref, wi_ref, cw_ref, c_ref, sa_ref, sb_ref,
                 yc_ref, qtz_ref, k_ref, vta_ref, gz_ref, qit_ref, ki_ref, sgt_ref, ubuf):
    j = pl.program_id(1)
    T = T_PROJ

    @pl.when(j == 0)
    def _():
        ubuf[0:8, :] = jnp.zeros((8, D_CONV), f32)

    x = x_ref[0]
    xn = x * lax.rsqrt(jnp.mean(x * x, axis=-1, keepdims=True) + EPS) * g_ref[...]
    xn = xn.astype(bf16)

    wc_ref, wa_ref = 0, 4

    def proj(first, k):
        return jnp.dot(xn, wm_ref[:, (first + k) * 512:(first + k + 1) * 512], preferred_element_type=f32)

    u = proj(wc_ref, 2) * proj(wc_ref, 0)
    ubuf[8:8 + T, :] = u
    conv = (ubuf[6:6 + T, :] * cw_ref[0:1, :] + ubuf[7:7 + T, :] * cw_ref[1:2, :] + u * cw_ref[2:3, :])
    ubuf[0:8, :] = ubuf[T:T + 8, :]
    zc = proj(wc_ref, 3)
    yc_ref[0] = (proj(wc_ref, 1) * conv * (zc * jax.nn.sigmoid(zc))).astype(bf16)

    c = c_ref[...]
    sa = sa_ref[...]
    sb = sb_ref[...]

    qf = proj(wa_ref, 0)
    kf = proj(wa_ref, 1)
    vf = proj(wa_ref, 2)
    sub = lax.broadcasted_iota(jnp.int32, (LANES, S_C), 0)
    chunks = [slice(r * S_C, (r + 1) * S_C) for r in range(T // S_C)]
    for i in range(D_ATTN // LANES):
        sl = slice(i * LANES, (i + 1) * LANES)
        qr = _rope128(qf[:, sl], c, sa, sb) * Q_SCALE
        k_ref[0, i] = _rope128(kf[:, sl], c, sa, sb).astype(bf16)
        for r, rs in enumerate(chunks):
            qt = qr[rs].T
            qtz_ref[0, r, 2 * i] = jnp.where(sub < HEAD_DIM, qt, 0.0).astype(bf16)
            qtz_ref[0, r, 2 * i + 1] = jnp.where(sub >= HEAD_DIM, qt, 0.0).astype(bf16)
            vt = vf[rs, sl].T.astype(bf16)
            for h in (2 * i, 2 * i + 1):
                vta_ref[0, r, h, 0:HEAD_DIM, :] = vt[(h % 2) * HEAD_DIM:(h % 2 + 1) * HEAD_DIM, :]
                vta_ref[0, r, h, HEAD_DIM:V_ROWS, :] = jnp.ones((V_ROWS - HEAD_DIM, S_C), bf16)
    za = proj(wa_ref, 3)
    gz_ref[0] = za * jax.nn.sigmoid(za)

    pi = jnp.dot(xn, wi_ref[...], preferred_element_type=f32)
    wv = pi[:, 3 * LANES:4 * LANES]
    sgn = jnp.where(wv > 0, 1.0, jnp.where(wv < 0, -1.0, 0.0))
    for r, rs in enumerate(chunks):
        sgt_ref[0, r] = sgn[rs].T[0:SUBLANES, :]
    wabs = jnp.abs(wv) * ((IDX_DIM ** -0.5) * (IDX_HEADS ** -0.5))
    lane = lax.broadcasted_iota(jnp.int32, (T, LANES), 1)
    for i in range(2):
        sl = slice(i * LANES, (i + 1) * LANES)
        scale = jnp.where(lane < IDX_DIM, wabs[:, 2 * i:2 * i + 1], wabs[:, 2 * i + 1:2 * i + 2])
        qi = _rope128(pi[:, sl], c, sa, sb) * scale
        for r, rs in enumerate(chunks):
            qit_ref[0, r, sl, :] = qi[rs].T.astype(bf16)
    ki_ref[0] = _rope128(pi[:, 2 * LANES:3 * LANES], c, sa, sb)[:, 0:IDX_DIM].astype(bf16)


def _attn_kernel(qtz_ref, qit_ref, sgt_ref, gz_ref, k_ref, vta_ref, ki_ref, up_ref, o_ref,
                 sc_ref, acc_ref, m_ref, l_ref, a0_ref, a1_ref, *, top_k, n_tiles):
    g = pl.program_id(1)
    TQ, SC = T_Q, S_C
    G = SC // SUBLANES
    nslot = sc_ref.shape[0] // 2
    nc = nslot - 1
    i = jnp.minimum(g, n_tiles - 1)
    ip = g - 1
    live = g < n_tiles
    cur = (g % 2) * nslot
    prev = nslot - cur

    def fold(a, op):
        return op(a.reshape(G, SUBLANES, TQ), axis=0)

    def rows(j):
        return pl.ds(pl.multiple_of(j * SC, SC), SC)

    def trips(n):
        return jnp.where(live, n, 0)

    t_loc = lax.broadcasted_iota(jnp.int32, (1, TQ), 1)
    lim_local = ((t_loc // CHUNK) + 1) * CHUNK
    limit = lim_local + i * TQ
    keff = jnp.minimum(limit, top_k).astype(f32)
    adm_diag = lax.broadcasted_iota(jnp.int32, (SC, TQ), 0) < lim_local

    sgt = sgt_ref[0, 0]
    sg = [sgt[h:h + 1, :] for h in range(IDX_HEADS)]

    def score_chunk(j):
        kc = ki_ref[0, rows(j), :]
        s = None
        for h in range(IDX_HEADS):
            l = jnp.dot(kc, qit_ref[0, 0, h * IDX_DIM:(h + 1) * IDX_DIM, :], preferred_element_type=f32)
            t = jnp.maximum(l, 0.0) * sg[h]
            s = t if s is None else s + t
        return s

    minus_inf = jnp.full((SC, TQ), -jnp.inf, f32)
    sc_ref[cur + nc] = minus_inf
    sc_ref[prev + nc] = minus_inf

    def p1_chunks(js, carry):
        mn, mx = carry
        for j in js:
            s = score_chunk(j)
            sc_ref[cur + j] = s
            mn, mx = jnp.minimum(mn, fold(s, jnp.min)), jnp.maximum(mx, fold(s, jnp.max))
        return mn, mx

    mn0 = jnp.full((SUBLANES, TQ), jnp.inf, f32)
    mx0 = jnp.full((SUBLANES, TQ), -jnp.inf, f32)
    U = 4
    carry = lax.fori_loop(0, trips(i // U), lambda jj, c: p1_chunks([U * jj + q for q in range(U)], c), (mn0, mx0))
    mn, mx = lax.fori_loop(trips(i // U * U), trips(i), lambda j, c: p1_chunks([j], c), carry)
    s = score_chunk(i)
    sc_ref[cur + i] = jnp.where(adm_diag, s, -jnp.inf)
    mn = jnp.minimum(mn, fold(jnp.where(adm_diag, s, jnp.inf), jnp.min))
    mx = jnp.maximum(mx, fold(jnp.where(adm_diag, s, -jnp.inf), jnp.max))
    lo = jnp.min(mn, axis=0, keepdims=True)
    hi = jnp.max(mx, axis=0, keepdims=True)

    acc_ref[...] = jnp.zeros(acc_ref.shape, f32)
    m_ref[...] = jnp.full(m_ref.shape, NEG, f32)
    l_ref[...] = jnp.zeros(l_ref.shape, f32)
    P = 2 * SUBLANES

    def all_sublanes(x8, op):
        for shift in (4, 2, 1):
            x8 = op(x8, pltpu.roll(x8, shift, 0))
        return x8

    def kv_chunk(c):
        return jnp.clip(c, 0, jnp.maximum(ip, 0))

    def logits_to(buf, c, between=lambda h: None):
        bias = sc_ref[jnp.where(c <= ip, prev + c, prev + nc)]
        for h in range(N_HEADS):
            kc = k_ref[0, h // 2, rows(kv_chunk(c)), :]
            buf[h] = (jnp.dot(kc, qtz_ref[0, 0, h], preferred_element_type=f32) + bias).astype(bf16)
            between(h)

    def softmax_pv(buf, c, between=lambda h: None):
        for h in range(N_HEADS):
            a = buf[h]
            cmax = jnp.max(a.reshape(SC // P, P, TQ), axis=0).astype(f32)
            m_old = m_ref[h]
            m_new = jnp.maximum(m_old, all_sublanes(jnp.maximum(cmax[0:SUBLANES], cmax[SUBLANES:P]), jnp.maximum))
            alpha = jnp.exp2(m_old - m_new)
            m_ref[h] = m_new
            mb = jnp.concatenate([m_new, m_new], axis=0).astype(bf16)
            p = jnp.exp2(a.reshape(SC // P, P, TQ) - mb[None]).reshape(SC, TQ)
            pv = jnp.dot(vta_ref[0, kv_chunk(c), h], p, preferred_element_type=f32)
            l_ref[h] = alpha * l_ref[h] + pv[HEAD_DIM:HEAD_DIM + SUBLANES]
            acc = acc_ref[h].reshape(HEAD_DIM // SUBLANES, SUBLANES, TQ) * alpha[None]
            acc_ref[h] = acc.reshape(HEAD_DIM, TQ) + pv[0:HEAD_DIM]
            between(h)

    def chunk_count(slot, mid8):
        return jnp.sum(jnp.where(sc_ref[slot].reshape(G, SUBLANES, TQ) >= mid8[None], 1.0, 0.0), axis=0)

    def update(state, mid, c, take=True):
        lo, hi, clo = state
        ge = c >= keff
        up = jnp.logical_and(ge, take)
        down = jnp.logical_and(jnp.logical_not(ge), take)
        return jnp.where(up, mid, lo), jnp.where(down, mid, hi), jnp.where(up, c, clo)

    n = i + 1
    n_trips = (g + 1) // 2
    per_step = (n + RIDE - 1) // RIDE
    n_ride = jnp.where(live, jnp.minimum(n_trips // per_step, N_BISECT), 0)
    logits_to(a0_ref, 0)

    def p4_body(jj, carry):
        lo, hi, clo, cnt, cursor, done = carry
        j = 2 * jj
        active = done < n_ride
        mid = 0.5 * lo + 0.5 * hi
        mid8 = jnp.broadcast_to(mid, (SUBLANES, TQ))

        counts = [cnt]

        def ride(stage):
            per_stage = RIDE // 4

            def between(h):
                for t in range(h * per_stage // N_HEADS, (h + 1) * per_stage // N_HEADS):
                    c = cursor + stage * per_stage + t
                    slot = jnp.where(jnp.logical_and(c <= i, active), cur + c, cur + nc)
                    counts[0] = counts[0] + chunk_count(slot, mid8)
            return between

        logits_to(a1_ref, j + 1, ride(0))
        softmax_pv(a0_ref, j, ride(1))
        logits_to(a0_ref, j + 2, ride(2))
        softmax_pv(a1_ref, j + 1, ride(3))
        cnt = counts[0]
        cursor = cursor + RIDE
        full = jnp.logical_and(cursor >= n, active)
        lo, hi, clo = update((lo, hi, clo), mid, jnp.sum(cnt, axis=0, keepdims=True), take=full)
        keep = jnp.where(full, 0.0, 1.0)
        return lo, hi, clo, cnt * keep, jnp.where(full, 0, cursor), done + full.astype(jnp.int32)

    clo = limit.astype(f32)
    lo, hi, clo, _, _, done = lax.fori_loop(
        0, n_trips, p4_body, (lo, hi, clo, jnp.zeros((SUBLANES, TQ), f32), jnp.int32(0), jnp.int32(0)))

    @pl.when(g >= 1)
    def _():
        for hp in range(N_HEADS // 2):
            pair = jnp.concatenate(
                [(acc_ref[h].reshape(HEAD_DIM // SUBLANES, SUBLANES, TQ) / l_ref[h][None]).reshape(HEAD_DIM, TQ)
                 for h in (2 * hp, 2 * hp + 1)], axis=0)
            sl = slice(hp * LANES, (hp + 1) * LANES)
            o_ref[0, :, sl] = (pair.T * gz_ref[0, :, sl]).astype(bf16)

    def count_ge(mid):
        mid8 = jnp.broadcast_to(mid, (SUBLANES, TQ))

        def some(js, cnt):
            for j in js:
                cnt = cnt + chunk_count(cur + j, mid8)
            return cnt
        cnt = lax.fori_loop(0, n // U, lambda jj, c: some([U * jj + q for q in range(U)], c),
                            jnp.zeros((SUBLANES, TQ), f32))
        cnt = lax.fori_loop(n // U * U, n, lambda j, c: some([j], c), cnt)
        return jnp.sum(cnt, axis=0, keepdims=True)

    def bisect(state):
        mid = 0.5 * state[0] + 0.5 * state[1]
        return update(state, mid, count_ge(mid))

    def snap(lo, hi):
        def some(js, carry):
            a, b = carry
            for j in js:
                s = sc_ref[cur + j]
                a = jnp.minimum(a, fold(jnp.where(s >= lo, s, jnp.inf), jnp.min))
                b = jnp.maximum(b, fold(jnp.where(s <= hi, s, -jnp.inf), jnp.max))
            return a, b
        carry = lax.fori_loop(0, trips(n // U), lambda jj, c: some([U * jj + q for q in range(U)], c), (mn0, mx0))
        a, b = lax.fori_loop(trips(n // U * U), trips(n), lambda j, c: some([j], c), carry)
        return jnp.min(a, axis=0, keepdims=True), jnp.max(b, axis=0, keepdims=True)

    lo, hi, clo = lax.fori_loop(0, trips(N_BISECT - done), lambda _, st: bisect(st), (lo, hi, clo))
    vlo, vhi = snap(lo, hi)

    def unconverged(lo, hi, clo, vlo, vhi):
        bad = jnp.logical_and(clo != keff, vlo != vhi)
        return jnp.logical_and(live, jnp.max(jnp.where(bad, 1.0, 0.0)) > 0.0)

    def refine(state):
        lo, hi, clo, _, _ = state
        lo, hi, clo = bisect((lo, hi, clo))
        vlo, vhi = snap(lo, hi)
        return lo, hi, clo, vlo, vhi

    lo, hi, clo, vlo, vhi = lax.while_loop(lambda st: unconverged(*st), refine, (lo, hi, clo, vlo, vhi))
    tau = vlo
    excess = clo - keff

    upper = up_ref[...]

    def p3_chunks(slots, carry):
        loaded = [(slot, sc_ref[slot]) for slot in slots]
        for slot, s in loaded:
            eq = s == tau
            eqf = jnp.where(eq, 1.0, 0.0)
            later = jnp.dot(upper, eqf.astype(bf16), preferred_element_type=f32) + carry
            sel = jnp.logical_or(s > tau, jnp.logical_and(eq, later >= excess))
            sc_ref[slot] = jnp.where(sel, 0.0, NEG)
            carry = later[0:1, :] + eqf[0:1, :]
        return carry

    rest = n % U
    carry = lax.fori_loop(0, trips(rest), lambda r, c: p3_chunks([cur + i - r], c), jnp.zeros((1, TQ), f32))
    half = (n // U) % 2
    carry = lax.fori_loop(0, trips(half),
                          lambda _, c: p3_chunks([cur + i - rest - q for q in range(U)], c), carry)
    top = i - rest - U * half
    lax.fori_loop(0, trips(n // (2 * U)),
                  lambda jj, c: p3_chunks([cur + top - 2 * U * jj - q for q in range(2 * U)], c), carry)


def _out_kernel(x_ref, yc_ref, ya_ref, woc_ref, woa_ref, g_ref, o_ref):
    y = (jnp.dot(yc_ref[0], woc_ref[...], preferred_element_type=f32)
         + jnp.dot(ya_ref[0], woa_ref[...], preferred_element_type=f32))
    yn = y * lax.rsqrt(jnp.mean(y * y, axis=-1, keepdims=True) + EPS) * g_ref[...]
    o_ref[0] = x_ref[0] + yn


def _rope_tables(S):
    half = ROT_DIM // 2
    inv_freq = jnp.power(jnp.float32(ROPE_THETA), -jnp.arange(half, dtype=f32) / half)
    ang = jnp.arange(S, dtype=jnp.int32).astype(f32)[:, None] * inv_freq[None, :]
    cos, sin = jnp.cos(ang), jnp.sin(ang)
    ones = jnp.ones((S, HEAD_DIM - ROT_DIM), f32)
    zeros = jnp.zeros((S, HEAD_DIM - ROT_DIM), f32)
    zh = jnp.zeros((S, half), f32)
    c = jnp.concatenate([cos, cos, ones], axis=1)
    sa = jnp.concatenate([-sin, zh, zeros], axis=1)
    sb = jnp.concatenate([zh, sin, zeros], axis=1)
    rep = LANES // HEAD_DIM
    return jnp.tile(c, (1, rep)), jnp.tile(sa, (1, rep)), jnp.tile(sb, (1, rep))


def _const_spec(shape):
    nd = len(shape)
    return pl.BlockSpec(shape, lambda b, j: (0,) * nd, pipeline_mode=pl.Buffered(1))


def _layer(x, g_pre, w_in, conv_w, w_out, g_post):
    B, S, _ = x.shape
    assert S % T_PROJ == 0 and S % T_Q == 0 and S % T_OUT == 0 and T_Q == S_C and T_PROJ % S_C == 0
    top_k = min(INDEX_TOPK, S // 4)
    nc = S // S_C

    o = 4 * D_CONV + 4 * D_ATTN
    wm = w_in[:, 0:o]
    nq = IDX_HEADS * IDX_DIM
    wi = jnp.zeros((D_MODEL, IDX_W), bf16)
    wi = wi.at[:, 0:nq + IDX_DIM].set(w_in[:, o:o + nq + IDX_DIM])
    wi = wi.at[:, 3 * LANES:3 * LANES + IDX_HEADS].set(w_in[:, o + nq + IDX_DIM:o + nq + IDX_DIM + IDX_HEADS])
    woc = w_out[0:D_CONV]
    woa = w_out[D_CONV:]
    c_t, sa_t, sb_t = _rope_tables(S)

    cp = functools.partial(pltpu.CompilerParams, vmem_limit_bytes=VMEM_LIMIT)

    tile = lambda w: pl.BlockSpec((1, T_PROJ, w), lambda b, j: (b, j, 0))
    ttile = lambda *shape: pl.BlockSpec((1, T_PROJ // S_C) + shape, lambda b, j: (b, j) + (0,) * len(shape))
    tab = pl.BlockSpec((T_PROJ, LANES), lambda b, j: (j, 0))
    sds = jax.ShapeDtypeStruct
    yc, qtz, k, vta, gz, qit, ki, sgt = pl.pallas_call(
        _proj_kernel,
        grid=(B, S // T_PROJ),
        in_specs=[tile(D_MODEL), _const_spec((1, D_MODEL)), _const_spec((D_MODEL, 4 * D_CONV + 4 * D_ATTN)),
                  _const_spec((D_MODEL, IDX_W)),
                  _const_spec((CONV_WIDTH, D_CONV)), tab, tab, tab],
        out_specs=[tile(D_CONV), ttile(N_HEADS, LANES, S_C),
                   pl.BlockSpec((1, D_ATTN // LANES, T_PROJ, LANES), lambda b, j: (b, 0, j, 0)),
                   ttile(N_HEADS, V_ROWS, S_C),
                   tile(D_ATTN), ttile(nq, S_C), tile(IDX_DIM), ttile(SUBLANES, S_C)],
        out_shape=[sds((B, S, D_CONV), bf16), sds((B, nc, N_HEADS, LANES, S_C), bf16),
                   sds((B, D_ATTN // LANES, S, LANES), bf16), sds((B, nc, N_HEADS, V_ROWS, S_C), bf16),
                   sds((B, S, D_ATTN), f32), sds((B, nc, nq, S_C), bf16),
                   sds((B, S, IDX_DIM), bf16), sds((B, nc, SUBLANES, S_C), f32)],
        scratch_shapes=[pltpu.VMEM((T_PROJ + 8, D_CONV), f32)],
        compiler_params=cp(dimension_semantics=("arbitrary", "arbitrary")),
        name="proj",
    )(x, g_pre.reshape(1, D_MODEL), wm, wi, conv_w, c_t, sa_t, sb_t)

    nq_tiles = S // T_Q
    lag = lambda g: jnp.maximum(g - 1, 0)
    sel = lambda g: jnp.minimum(g, nq_tiles - 1)
    whole = lambda *shape: pl.BlockSpec((1,) + shape, lambda b, g: (b,) + (0,) * len(shape),
                                        pipeline_mode=pl.Buffered(1))
    ya = pl.pallas_call(
        functools.partial(_attn_kernel, top_k=top_k, n_tiles=nq_tiles),
        grid=(B, nq_tiles + 1),
        in_specs=[pl.BlockSpec((1, 1, N_HEADS, LANES, T_Q), lambda b, g: (b, lag(g), 0, 0, 0)),
                  pl.BlockSpec((1, 1, nq, T_Q), lambda b, g: (b, sel(g), 0, 0)),
                  pl.BlockSpec((1, 1, SUBLANES, T_Q), lambda b, g: (b, sel(g), 0, 0)),
                  pl.BlockSpec((1, T_Q, D_ATTN), lambda b, g: (b, lag(g), 0)),
                  whole(D_ATTN // LANES, S, LANES), whole(nc, N_HEADS, V_ROWS, S_C), whole(S, IDX_DIM),
                  _const_spec((S_C, S_C))],
        out_specs=pl.BlockSpec((1, T_Q, D_ATTN), lambda b, g: (b, lag(g), 0)),
        out_shape=sds((B, S, D_ATTN), bf16),
        scratch_shapes=[pltpu.VMEM((2 * (nc + 1), S_C, T_Q), f32), pltpu.VMEM((N_HEADS, HEAD_DIM, T_Q), f32),
                        pltpu.VMEM((N_HEADS, SUBLANES, T_Q), f32), pltpu.VMEM((N_HEADS, SUBLANES, T_Q), f32),
                        pltpu.VMEM((N_HEADS, S_C, T_Q), bf16), pltpu.VMEM((N_HEADS, S_C, T_Q), bf16)],
        compiler_params=cp(dimension_semantics=("arbitrary", "arbitrary")),
        name="attn",
    )(qtz, qit, sgt, gz, k, vta, ki, jnp.triu(jnp.ones((S_C, S_C), bf16), 1))

    otile = lambda w: pl.BlockSpec((1, T_OUT, w), lambda b, j: (b, j, 0))
    return pl.pallas_call(
        _out_kernel,
        grid=(B, S // T_OUT),
        in_specs=[otile(D_MODEL), otile(D_CONV), otile(D_ATTN), _const_spec((D_CONV, D_MODEL)),
                  _const_spec((D_ATTN, D_MODEL)), _const_spec((1, D_MODEL))],
        out_specs=otile(D_MODEL),
        out_shape=sds((B, S, D_MODEL), f32),
        compiler_params=cp(dimension_semantics=("arbitrary", "arbitrary")),
        name="outproj",
    )(x, yc, ya, woc, woa, g_post.reshape(1, D_MODEL))


def kernel(x, norm_pre_g, w_in, conv_w, w_out, norm_post_g):
    w_in, w_out = w_in.astype(bf16), w_out.astype(bf16)
    for l in range(norm_pre_g.shape[0]):
        x = _layer(x, norm_pre_g[l], w_in[l], conv_w[l], w_out[l], norm_post_g[l])
    return x
```
